```python
import math
import jax, jax.numpy as jnp
from jax import lax
import numpy as np

D_MODEL = 1024
BATCH = 2
SEQ = 8192
DEPTH = 1

CTX_LEN = 256
GRID_W = 64
MIX_W = D_MODEL
ATTN_W = MIX_W // 2
ATTN_HEAD_DIM = 64
ATTN_HEADS = ATTN_W // ATTN_HEAD_DIM
ATTN_KV_HEADS = ATTN_HEADS // 4
ATTN_GROUP = ATTN_HEADS // ATTN_KV_HEADS
ATTN_KV_W = ATTN_KV_HEADS * ATTN_HEAD_DIM
DN_W = MIX_W - ATTN_W
DN_HEAD_DIM = 128
DN_HEADS = DN_W // DN_HEAD_DIM
N_DIR = 2
CONV_K = 5
CHUNK = 64
Q_BLOCK = 128
D_FF = 4 * D_MODEL
N_MOD = 6
ROPE_THETA = 10000.0
NORM_EPS = 1e-6
IN_COLS = ATTN_W + 2 * ATTN_KV_W + 4 * DN_W + 2 * N_DIR * DN_HEADS

kernel_name = 'hybrid_attn_gdn_dit_block'


def _rms(x, w):
    xf = x.astype(jnp.float32)
    y = xf * lax.rsqrt(jnp.mean(xf * xf, axis=-1, keepdims=True) + NORM_EPS)
    return (y * w.astype(jnp.float32)).astype(x.dtype)


def _l2n(x):
    return x * lax.rsqrt(jnp.sum(x * x, axis=-1, keepdims=True) + NORM_EPS)


def _rope_axis(x, pos):
    half = x.shape[-1] // 2
    inv = ROPE_THETA ** (-jnp.arange(half, dtype=jnp.float32) / half)
    ang = pos.astype(jnp.float32)[:, None] * inv[None, :]
    cos = jnp.cos(ang)[None, :, None, :]
    sin = jnp.sin(ang)[None, :, None, :]
    x1, x2 = x[..., :half], x[..., half:]
    return jnp.concatenate([x1 * cos - x2 * sin, x2 * cos + x1 * sin], axis=-1)


def _rope_2d(x, row, col):
    xf = x.astype(jnp.float32)
    h = x.shape[-1] // 2
    y = jnp.concatenate([_rope_axis(xf[..., :h], row), _rope_axis(xf[..., h:], col)], axis=-1)
    return y.astype(x.dtype)


def _split_proj(p):
    widths = (ATTN_W, ATTN_KV_W, ATTN_KV_W, 3 * DN_W, DN_W, N_DIR * DN_HEADS, N_DIR * DN_HEADS)
    out, off = [], 0
    for wdt in widths:
        out.append(p[..., off:off + wdt])
        off += wdt
    return out


def _short_conv(x, w):
    y = lax.conv_general_dilated(
        x, w[:, None, :].astype(x.dtype), window_strides=(1,),
        padding=[(CONV_K // 2, CONV_K // 2)],
        dimension_numbers=('NWC', 'WIO', 'NWC'), feature_group_count=x.shape[-1])
    return jax.nn.silu(y)


def _gdn_gates(b_raw, a_raw, a_log, dt_bias):
    shp = b_raw.shape[:-1] + (N_DIR, DN_HEADS)
    beta = jax.nn.sigmoid(b_raw.astype(jnp.float32)).reshape(shp)
    g = -jnp.exp(a_log.astype(jnp.float32)) * jax.nn.softplus(
        a_raw.astype(jnp.float32).reshape(shp) + dt_bias.astype(jnp.float32))
    return beta, g


def _dir(a, d):
    return a[:, ::-1] if d == 1 else a


def _gdn_chunked(k, v, g, beta, s0, q=None):
    b, t, h, dk = k.shape
    n = t // CHUNK

    def chunks(a):
        a = a.reshape((b, n, CHUNK, h) + a.shape[3:])
        return jnp.moveaxis(jnp.moveaxis(a, 1, 0), 3, 2)

    kc, vc, bc = chunks(k), chunks(v), chunks(beta)
    gc = jnp.cumsum(chunks(g), axis=-1)
    idx = jnp.arange(CHUNK)
    incl = idx[:, None] >= idx[None, :]
    strict = idx[:, None] > idx[None, :]
    decay = jnp.exp(jnp.where(incl, gc[..., :, None] - gc[..., None, :], -jnp.inf))
    kb = kc * bc[..., None]
    lower = jnp.where(strict, jnp.einsum('nbhid,nbhjd->nbhij', kb, kc) * decay, 0.0)
    eye = jnp.eye(CHUNK, dtype=jnp.float32)
    t_inv = lax.linalg.triangular_solve(eye + lower, jnp.broadcast_to(eye, lower.shape),
                                        left_side=True, lower=True, unit_diagonal=True)
    u = t_inv @ (vc * bc[..., None])
    w = t_inv @ (kb * jnp.exp(gc)[..., None])
    g_last = gc[..., -1]
    k_dec = kc * jnp.exp(g_last[..., None] - gc)[..., None]
    xs = (u, w, k_dec, g_last)
    if q is not None:
        qs = chunks(q) * (dk ** -0.5)
        xs = xs + (qs * jnp.exp(gc)[..., None], jnp.einsum('nbhid,nbhjd->nbhij', qs, kc) * decay)

    def step(s, xs_i):
        u_i, w_i, kd_i, gl_i = xs_i[:4]
        v_new = u_i - jnp.einsum('bhcd,bhde->bhce', w_i, s)
        s_new = s * jnp.exp(gl_i)[..., None, None] + jnp.einsum('bhcd,bhce->bhde', kd_i, v_new)
        if q is None:
            return s_new, None
        qd_i, intra_i = xs_i[4:]
        o_i = jnp.einsum('bhcd,bhde->bhce', qd_i, s) + jnp.einsum('bhij,bhje->bhie', intra_i, v_new)
        return s_new, o_i

    s_final, o = lax.scan(step, s0, xs)
    if q is None:
        return None, s_final
    return o.transpose(1, 0, 3, 2, 4).reshape(b, t, h, v.shape[-1]), s_final


def _latent_attention(q, k_lat, v_lat, k_ctx, v_ctx):
    b, t, _, hd = q.shape
    k_all = jnp.concatenate([k_ctx, k_lat], axis=1)
    v_all = jnp.concatenate([v_ctx, v_lat], axis=1)
    n_blk = t // Q_BLOCK
    qb = q.reshape(b, n_blk, Q_BLOCK, ATTN_KV_HEADS, ATTN_GROUP, hd).transpose(1, 0, 2, 3, 4, 5)
    scale = hd ** -0.5

    def one_block(q_blk):
        s = jnp.einsum('bqkgd,bskd->bkgqs', q_blk, k_all).astype(jnp.float32) * scale
        p = jax.nn.softmax(s, axis=-1).astype(v_all.dtype)
        return jnp.einsum('bkgqs,bskd->bqkgd', p, v_all)

    o = lax.map(one_block, qb)
    return o.transpose(1, 0, 2, 3, 4, 5).reshape(b, t, ATTN_W)


def setup_inputs(seed: int = 0) -> dict:
    key = jax.random.key(seed)
    ks = jax.random.split(key, 20)
    f32 = jnp.float32

    def nrm(k, shape, scale):
        return jax.random.normal(k, shape, f32) * scale

    dt = jnp.exp(jax.random.uniform(ks[12], (DEPTH, N_DIR, DN_HEADS), f32,
                                    math.log(1e-3), math.log(1e-1)))
    return {
        'x': nrm(ks[0], (BATCH, SEQ, D_MODEL), 1.0),
        'c': nrm(ks[1], (BATCH, D_MODEL), 1.0),
        'ctx': nrm(ks[2], (BATCH, CTX_LEN, D_MODEL), 1.0),
        'c_ctx': nrm(ks[3], (D_MODEL,), 1.0),
        'w_mod': nrm(ks[4], (DEPTH, D_MODEL, N_MOD * D_MODEL), 0.5 * D_MODEL ** -0.5),
        'b_mod': nrm(ks[5], (DEPTH, N_MOD * D_MODEL), 0.01),
        'norm1_w': 1.0 + nrm(ks[6], (DEPTH, D_MODEL), 0.02),
        'w_in': nrm(ks[7], (DEPTH, D_MODEL, IN_COLS), D_MODEL ** -0.5),
        'q_norm_w': 1.0 + nrm(ks[8], (DEPTH, ATTN_HEAD_DIM), 0.02),
        'k_norm_w': 1.0 + nrm(ks[9], (DEPTH, ATTN_HEAD_DIM), 0.02),
        'conv_w': nrm(ks[10], (DEPTH, CONV_K, 3 * DN_W), CONV_K ** -0.5),
        'a_log': jnp.log(jax.random.uniform(ks[11], (DEPTH, N_DIR, DN_HEADS), f32, 1.0, 16.0)),
        'dt_bias': dt + jnp.log(-jnp.expm1(-dt)),
        'dn_norm_w': 1.0 + nrm(ks[13], (DEPTH, DN_HEAD_DIM), 0.02),
        'w_out': nrm(ks[14], (DEPTH, MIX_W, D_MODEL), MIX_W ** -0.5),
        'norm2_w': 1.0 + nrm(ks[15], (DEPTH, D_MODEL), 0.02),
        'w_mlp1': nrm(ks[16], (DEPTH, D_MODEL, D_FF), D_MODEL ** -0.5),
        'w_mlp2': nrm(ks[17], (DEPTH, D_FF, D_MODEL), D_FF ** -0.5),
    }


def reference(x, c, ctx, c_ctx, w_mod, b_mod, norm1_w, w_in, q_norm_w, k_norm_w, conv_w,
              a_log, dt_bias, dn_norm_w, w_out, norm2_w, w_mlp1, w_mlp2):
    f32 = jnp.float32
    b, t, _ = x.shape
    n_ctx = ctx.shape[1]
    rows = t // GRID_W
    row = jnp.repeat(jnp.arange(rows, dtype=jnp.int32), GRID_W, total_repeat_length=rows * GRID_W)
    col = jnp.arange(t, dtype=jnp.int32) % GRID_W
    silu_c = jax.nn.silu(c)
    silu_cc = jax.nn.silu(c_ctx)
    for l in range(DEPTH):
        mod = (silu_c @ w_mod[l] + b_mod[l]).reshape(b, N_MOD, D_MODEL)[:, :, None, :]
        sh1, sc1, g1, sh2, sc2, g2 = (mod[:, i] for i in range(N_MOD))
        mod_ctx = (silu_cc @ w_mod[l] + b_mod[l]).reshape(N_MOD, D_MODEL)

        h = _rms(x, norm1_w[l]) * (1.0 + sc1) + sh1
        hc = _rms(ctx, norm1_w[l]) * (1.0 + mod_ctx[1]) + mod_ctx[0]
        aq, ak, av, dqkv, dz, db, da = _split_proj(h @ w_in[l])
        _, cak, cav, cdqkv, _, cdb, cda = _split_proj(hc @ w_in[l])

        q_a = _rope_2d(_rms(aq.reshape(b, t, ATTN_HEADS, ATTN_HEAD_DIM), q_norm_w[l]), row, col)
        k_a = _rope_2d(_rms(ak.reshape(b, t, ATTN_KV_HEADS, ATTN_HEAD_DIM), k_norm_w[l]), row, col)
        v_a = av.reshape(b, t, ATTN_KV_HEADS, ATTN_HEAD_DIM)
        k_c = _rms(cak.reshape(b, n_ctx, ATTN_KV_HEADS, ATTN_HEAD_DIM), k_norm_w[l])
        v_c = cav.reshape(b, n_ctx, ATTN_KV_HEADS, ATTN_HEAD_DIM)
        o_attn = _latent_attention(q_a, k_a, v_a, k_c, v_c)

        qkv = _short_conv(dqkv, conv_w[l]).astype(f32).reshape(b, t, 3, DN_HEADS, DN_HEAD_DIM)
        dn_q, dn_k, dn_v = _l2n(qkv[:, :, 0]), _l2n(qkv[:, :, 1]), qkv[:, :, 2]
        kv_c = _short_conv(cdqkv[..., DN_W:], conv_w[l][:, DN_W:]).astype(f32)
        kv_c = kv_c.reshape(b, n_ctx, 2, DN_HEADS, DN_HEAD_DIM)
        ck, cv = _l2n(kv_c[:, :, 0]), kv_c[:, :, 1]
        beta, gdec = _gdn_gates(db, da, a_log[l], dt_bias[l])
        cbeta, cgdec = _gdn_gates(cdb, cda, a_log[l], dt_bias[l])
        s_zero = jnp.zeros((b, DN_HEADS, DN_HEAD_DIM, DN_HEAD_DIM), f32)
        o_dn = jnp.zeros((b, t, DN_HEADS, DN_HEAD_DIM), f32)
        for d in range(N_DIR):
            _, s_ctx = _gdn_chunked(_dir(ck, d), _dir(cv, d), _dir(cgdec[:, :, d], d),
                                    _dir(cbeta[:, :, d], d), s_zero)
            o_d, _ = _gdn_chunked(_dir(dn_k, d), _dir(dn_v, d), _dir(gdec[:, :, d], d),
                                  _dir(beta[:, :, d], d), s_ctx, q=_dir(dn_q, d))
            o_dn = o_dn + _dir(o_d, d)
        z = dz.astype(f32).reshape(b, t, DN_HEADS, DN_HEAD_DIM)
        o_dn = (_rms(o_dn, dn_norm_w[l]) * jax.nn.silu(z)).reshape(b, t, DN_W).astype(x.dtype)

        mixed = jnp.concatenate([o_attn, o_dn], axis=-1) @ w_out[l]
        x = x + g1 * mixed

        h2 = _rms(x, norm2_w[l]) * (1.0 + sc2) + sh2
        x = x + g2 * (jnp.square(jax.nn.relu(h2 @ w_mlp1[l])) @ w_mlp2[l])
    return x
```

```python
import functools
import math

import jax
import jax.numpy as jnp
from jax import lax
from jax.experimental import pallas as pl
from jax.experimental.pallas import tpu as pltpu

F32 = jnp.float32
BF16 = jnp.bfloat16

GRID_W = 64
ATTN_HEAD_DIM = 64
ATTN_HEADS = 8
ATTN_KV_HEADS = 2
ATTN_GROUP = ATTN_HEADS // ATTN_KV_HEADS
ATTN_W = ATTN_HEADS * ATTN_HEAD_DIM
ATTN_KV_W = ATTN_KV_HEADS * ATTN_HEAD_DIM
DN_HEAD_DIM = 128
DN_HEADS = 4
DN_W = DN_HEADS * DN_HEAD_DIM
N_DIR = 2
CONV_K = 5
CHUNK = 64
N_MOD = 6
ROPE_THETA = 10000.0
NORM_EPS = 1e-6
LANES = 128
GATE_W = LANES
N_GATE = N_DIR * DN_HEADS

OFF_Q = 0
OFF_K = OFF_Q + ATTN_W
OFF_V = OFF_K + ATTN_KV_W
OFF_DQKV = OFF_V + ATTN_KV_W
OFF_Z = OFF_DQKV + 3 * DN_W
OFF_GATE = OFF_Z + DN_W
IN_COLS_PAD = OFF_GATE + GATE_W

VMEM_LIMIT = 56 * 1024 * 1024


def _cparams(sem):
    return pltpu.CompilerParams(dimension_semantics=sem, vmem_limit_bytes=VMEM_LIMIT)


def _dot_bf16(a, b):
    return jnp.dot(a.astype(BF16), b.astype(BF16), preferred_element_type=F32)


def _dot_f32(a, b):
    return jnp.dot(a, b, preferred_element_type=F32, precision=lax.Precision.HIGHEST)


def _split_bf16(a):
    hi = a.astype(BF16)
    lo = (a - hi.astype(F32)).astype(BF16)
    return hi, lo


def _dot_x3(a, b):
    ah, al = _split_bf16(a)
    bh, bl = _split_bf16(b)
    d = functools.partial(jnp.dot, preferred_element_type=F32)
    return d(ah, bh) + (d(ah, bl) + d(al, bh))


def _silu(x):
    return x * jax.nn.sigmoid(x)


def _mod_kernel(c_ref, w_ref, b_ref, o_ref):
    o_ref[...] = _dot_f32(_silu(c_ref[...]), w_ref[...]) + b_ref[...]


def _modulation(c_rows, w_mod, b_mod):
    r, d = c_rows.shape
    n = w_mod.shape[1]
    tn = 1024
    return pl.pallas_call(
        _mod_kernel,
        grid=(n // tn,),
        in_specs=[pl.BlockSpec((r, d), lambda j: (0, 0)),
                  pl.BlockSpec((d, tn), lambda j: (0, j)),
                  pl.BlockSpec((1, tn), lambda j: (0, j))],
        out_specs=pl.BlockSpec((r, tn), lambda j: (0, j)),
        out_shape=jax.ShapeDtypeStruct((r, n), F32),
        compiler_params=_cparams(("arbitrary",)),
        name="modulation",
    )(c_rows, w_mod, b_mod.reshape(1, n))


def _group_mean_sq(a, gmat, group):
    hi, lo = _split_bf16(a * a)
    d = functools.partial(jnp.dot, preferred_element_type=F32)
    return (d(hi, gmat) + d(lo, gmat)) * (1.0 / group)


def _rope(a, cos, sin):
    parts = []
    for c in range(a.shape[1] // LANES):
        ac = a[:, c * LANES:(c + 1) * LANES]
        lane = lax.broadcasted_iota(jnp.int32, ac.shape, 1)
        nxt = pltpu.roll(ac, LANES - 16, 1)
        prv = pltpu.roll(ac, 16, 1)
        parts.append(jnp.where((lane & 31) < 16, nxt, prv))
    partner = parts[0] if len(parts) == 1 else jnp.concatenate(parts, axis=1)
    return a * cos + partner * sin


def _gate_maps(raw, alog, dtb):
    lane = lax.broadcasted_iota(jnp.int32, raw.shape, 1)
    beta = jax.nn.sigmoid(raw)
    y = raw + dtb
    softplus = jnp.maximum(y, 0.0) + jnp.log1p(jnp.exp(-jnp.abs(y)))
    g = -jnp.exp(alog) * softplus
    return jnp.where(lane < N_GATE, beta, g)


def _inproj_kernel(is_ctx, x_ref, mod_ref, n1w_ref, w_ref, qnw_ref, knw_ref, cos_ref, sin_ref,
                   gmat_ref, alog_ref, dtb_ref, *outs):
    x = x_ref[...]
    y = x * lax.rsqrt(jnp.mean(x * x, axis=-1, keepdims=True) + NORM_EPS) * n1w_ref[...]
    h = (y * (1.0 + mod_ref[1:2, :]) + mod_ref[0:1, :]).astype(BF16)

    def proj(lo, hi):
        return jnp.dot(h, w_ref[:, lo:hi], preferred_element_type=F32)

    kk = proj(OFF_K, OFF_V)
    kk = kk * lax.rsqrt(_group_mean_sq(kk, gmat_ref[:ATTN_KV_W, :ATTN_KV_W], ATTN_HEAD_DIM)
                        + NORM_EPS) * knw_ref[...]
    vv = proj(OFF_V, OFF_DQKV)
    gates = _gate_maps(proj(OFF_GATE, IN_COLS_PAD), alog_ref[...], dtb_ref[...])
    if is_ctx:
        k_out, v_out, dkv_out, gate_out = outs
        dkv_out[...] = proj(OFF_DQKV + DN_W, OFF_Z)
    else:
        q_out, k_out, v_out, dqkv_out, sz_out, gate_out = outs
        cos = cos_ref[...]
        sin = sin_ref[...]
        kk = _rope(kk, cos, sin)
        qq = proj(OFF_Q, OFF_K)
        qq = qq * lax.rsqrt(_group_mean_sq(qq, gmat_ref[...], ATTN_HEAD_DIM) + NORM_EPS) * qnw_ref[...]
        rep = ATTN_W // LANES
        qq = _rope(qq, jnp.concatenate([cos] * rep, axis=1), jnp.concatenate([sin] * rep, axis=1))
        q_out[...] = (qq * (ATTN_HEAD_DIM ** -0.5)).astype(q_out.dtype)
        dqkv_out[...] = proj(OFF_DQKV, OFF_Z)
        sz_out[...] = _silu(proj(OFF_Z, OFF_GATE))
    k_out[...] = kk.astype(k_out.dtype)
    v_out[...] = vv.astype(v_out.dtype)
    gate_out[...] = gates


def _inproj(x, mod, mod_index, is_ctx, tm, consts):
    b, t, d = x.shape
    n1w, w_in, qnw, knw, cos, sin, gmat, alog, dtb = consts
    nt = t // tm
    row = lambda bi, i: (bi, i, 0)
    full = lambda bi, i: (0, 0)
    in_specs = [
        pl.BlockSpec((None, tm, d), row),
        pl.BlockSpec((None, N_MOD, d), mod_index),
        pl.BlockSpec((1, d), full),
        pl.BlockSpec(w_in.shape, full),
        pl.BlockSpec(qnw.shape, full),
        pl.BlockSpec(knw.shape, full),
        pl.BlockSpec((tm, LANES), lambda bi, i: (i, 0)),
        pl.BlockSpec((tm, LANES), lambda bi, i: (i, 0)),
        pl.BlockSpec(gmat.shape, full),
        pl.BlockSpec((1, GATE_W), full),
        pl.BlockSpec((1, GATE_W), full),
    ]

    def out(width, dtype):
        return pl.BlockSpec((None, tm, width), row), jax.ShapeDtypeStruct((b, t, width), dtype)

    if is_ctx:
        outs = [out(ATTN_KV_W, BF16), out(ATTN_KV_W, BF16), out(2 * DN_W, F32), out(GATE_W, F32)]
    else:
        outs = [out(ATTN_W, BF16), out(ATTN_KV_W, BF16), out(ATTN_KV_W, BF16),
                out(3 * DN_W, F32), out(DN_W, F32), out(GATE_W, F32)]
    return pl.pallas_call(
        functools.partial(_inproj_kernel, is_ctx),
        grid=(b, nt),
        in_specs=in_specs,
        out_specs=[o[0] for o in outs],
        out_shape=[o[1] for o in outs],
        compiler_params=_cparams(("parallel", "parallel")),
        name="inproj_ctx" if is_ctx else "inproj",
    )(x, mod, n1w, w_in, qnw, knw, cos, sin, gmat, alog, dtb)


HALO = 8


def _conv_kernel(n_norm, x_ref, prev_ref, next_ref, w_ref, o_ref, win_ref):
    i = pl.program_id(1)
    nt = pl.num_programs(1)
    tc = x_ref.shape[0]
    win_ref[0:HALO, :] = jnp.where(i > 0, prev_ref[...], 0.0)
    win_ref[HALO:HALO + tc, :] = x_ref[...]
    win_ref[HALO + tc:, :] = jnp.where(i < nt - 1, next_ref[...], 0.0)
    for h in range(x_ref.shape[1] // LANES):
        cols = slice(h * LANES, (h + 1) * LANES)
        acc = None
        for j in range(CONV_K):
            start = HALO - CONV_K // 2 + j
            term = win_ref[start:start + tc, cols] * w_ref[j:j + 1, cols]
            acc = term if acc is None else acc + term
        y = _silu(acc)
        if h < n_norm:
            y = y * lax.rsqrt(jnp.sum(y * y, axis=-1, keepdims=True) + NORM_EPS)
        o_ref[:, cols] = y


def _short_conv_norm(x, w, n_norm, tc):
    b, t, c = x.shape
    nt = t // tc
    per = tc // HALO
    last = t // HALO - 1
    return pl.pallas_call(
        functools.partial(_conv_kernel, n_norm),
        grid=(b, nt),
        in_specs=[
            pl.BlockSpec((None, tc, c), lambda bi, i: (bi, i, 0)),
            pl.BlockSpec((None, HALO, c), lambda bi, i: (bi, jnp.maximum(i * per - 1, 0), 0)),
            pl.BlockSpec((None, HALO, c), lambda bi, i: (bi, jnp.minimum((i + 1) * per, last), 0)),
            pl.BlockSpec(w.shape, lambda bi, i: (0, 0)),
        ],
        out_specs=pl.BlockSpec((None, tc, c), lambda bi, i: (bi, i, 0)),
        out_shape=jax.ShapeDtypeStruct((b, t, c), F32),
        scratch_shapes=[pltpu.VMEM((tc + 2 * HALO, c), F32)],
        compiler_params=_cparams(("parallel", "parallel")),
        name="short_conv",
    )(x, x, x, w)


def _attn_kernel(q_ref, kc_ref, vc_ref, k_ref, v_ref, o_ref, m_ref, l_ref, acc_ref):
    ki = pl.program_id(2)
    nk = pl.num_programs(2)

    def process(kb, vb):
        for j in range(ATTN_KV_HEADS):
            kj = kb[:, j * ATTN_HEAD_DIM:(j + 1) * ATTN_HEAD_DIM]
            vj = vb[:, j * ATTN_HEAD_DIM:(j + 1) * ATTN_HEAD_DIM]
            for g in range(ATTN_GROUP):
                h = j * ATTN_GROUP + g
                qh = q_ref[:, h * ATTN_HEAD_DIM:(h + 1) * ATTN_HEAD_DIM]
                s = lax.dot_general(qh, kj, (((1,), (1,)), ((), ())), preferred_element_type=F32)
                m_prev = m_ref[h]
                m_new = jnp.maximum(m_prev, jnp.max(s, axis=-1, keepdims=True))
                alpha = jnp.exp(m_prev - m_new)
                p = jnp.exp(s - m_new)
                l_ref[h] = alpha * l_ref[h] + jnp.sum(p, axis=-1, keepdims=True)
                acc_ref[h] = alpha * acc_ref[h] + jnp.dot(p.astype(BF16), vj, preferred_element_type=F32)
                m_ref[h] = m_new

    @pl.when(ki == 0)
    def _():
        m_ref[...] = jnp.full(m_ref.shape, -jnp.inf, F32)
        l_ref[...] = jnp.zeros(l_ref.shape, F32)
        acc_ref[...] = jnp.zeros(acc_ref.shape, F32)
        process(kc_ref[...], vc_ref[...])

    process(k_ref[...], v_ref[...])

    @pl.when(ki == nk - 1)
    def _():
        for h in range(ATTN_HEADS):
            o_ref[:, h * ATTN_HEAD_DIM:(h + 1) * ATTN_HEAD_DIM] = (acc_ref[h] / l_ref[h]).astype(o_ref.dtype)


def _attention(q, k, v, kc, vc, tq, tk):
    b, t, _ = q.shape
    n_ctx = kc.shape[1]
    return pl.pallas_call(
        _attn_kernel,
        grid=(b, t // tq, t // tk),
        in_specs=[
            pl.BlockSpec((None, tq, ATTN_W), lambda bi, qi, ki: (bi, qi, 0)),
            pl.BlockSpec((None, n_ctx, ATTN_KV_W), lambda bi, qi, ki: (bi, 0, 0)),
            pl.BlockSpec((None, n_ctx, ATTN_KV_W), lambda bi, qi, ki: (bi, 0, 0)),
            pl.BlockSpec((None, tk, ATTN_KV_W), lambda bi, qi, ki: (bi, ki, 0)),
            pl.BlockSpec((None, tk, ATTN_KV_W), lambda bi, qi, ki: (bi, ki, 0)),
        ],
        out_specs=pl.BlockSpec((None, tq, ATTN_W), lambda bi, qi, ki: (bi, qi, 0)),
        out_shape=jax.ShapeDtypeStruct((b, t, ATTN_W), BF16),
        scratch_shapes=[pltpu.VMEM((ATTN_HEADS, tq, 1), F32),
                        pltpu.VMEM((ATTN_HEADS, tq, 1), F32),
                        pltpu.VMEM((ATTN_HEADS, tq, ATTN_HEAD_DIM), F32)],
        compiler_params=_cparams(("parallel", "parallel", "arbitrary")),
        name="attention",
    )(q, kc, vc, k, v)


def _unit_tri_inverse(n_strict, row, col):
    eye = (row == col).astype(F32)
    d = eye
    s = 1
    while s < CHUNK:
        sh = s.bit_length() - 1
        join = ((row >> (sh + 1)) == (col >> (sh + 1))) & ((row >> sh) != (col >> sh))
        e = jnp.where(join, n_strict, 0.0)
        if s == 1:
            d = eye - e
        else:
            d = d - _dot_x3(d, _dot_x3(e, d))
        s *= 2
    return d


def _gdn_kernel(has_q, *refs):
    if has_q:
        (xf_ref, xb_ref, gf_ref, gb_ref, s0_ref, of_ref, ob_ref, sfin_ref, s_ref) = refs
    else:
        (xf_ref, xb_ref, gf_ref, gb_ref, s0_ref, sfin_ref, s_ref) = refs
    i = pl.program_id(1)
    nc = pl.num_programs(1)

    @pl.when(i == 0)
    def _():
        s_ref[...] = s0_ref[...]

    row = lax.broadcasted_iota(jnp.int32, (CHUNK, CHUNK), 0)
    col = lax.broadcasted_iota(jnp.int32, (CHUNK, CHUNK), 1)
    off_k = DN_W if has_q else 0
    off_v = off_k + DN_W
    for d in range(N_DIR):
        x_ref, g_ref = (xf_ref, gf_ref) if d == 0 else (xb_ref, gb_ref)
        incl = (row >= col) if d == 0 else (row <= col)
        incl_t = (col >= row) if d == 0 else (col <= row)
        strict = (row > col) if d == 0 else (row < col)
        last = CHUNK - 1 if d == 0 else 0
        gates = g_ref[...]
        csum = _dot_f32(incl.astype(F32), gates)
        csum_t = _dot_f32(gates.T, incl_t.astype(F32))
        for h in range(DN_HEADS):
            u_idx = d * DN_HEADS + h
            beta = gates[:, u_idx:u_idx + 1]
            gc = csum[:, N_GATE + u_idx:N_GATE + u_idx + 1]
            gc_t = csum_t[N_GATE + u_idx:N_GATE + u_idx + 1, :]
            g_last = gc[last:last + 1, :]
            k = x_ref[:, off_k + h * DN_HEAD_DIM:off_k + (h + 1) * DN_HEAD_DIM]
            v = x_ref[:, off_v + h * DN_HEAD_DIM:off_v + (h + 1) * DN_HEAD_DIM]
            decay = jnp.exp(jnp.where(incl, gc - gc_t, -jnp.inf))
            kb = k * beta
            kb16 = k.astype(BF16)
            kk = lax.dot_general(kb.astype(BF16), kb16, (((1,), (1,)), ((), ())),
                                 preferred_element_type=F32)
            t_inv = _unit_tri_inverse(jnp.where(strict, kk * decay, 0.0), row, col)
            rhs = jnp.concatenate([v * beta, kb * jnp.exp(gc)], axis=1)
            uw = _dot_bf16(t_inv, rhs)
            u = uw[:, :DN_HEAD_DIM]
            w = uw[:, DN_HEAD_DIM:]
            k_dec = k * jnp.exp(g_last - gc)
            s_old = s_ref[u_idx]
            v_new = u - _dot_bf16(w, s_old)
            s_ref[u_idx] = s_old * jnp.exp(g_last) + lax.dot_general(
                k_dec.astype(BF16), v_new.astype(BF16), (((0,), (0,)), ((), ())),
                preferred_element_type=F32)
            if has_q:
                q = x_ref[:, h * DN_HEAD_DIM:(h + 1) * DN_HEAD_DIM] * (DN_HEAD_DIM ** -0.5)
                intra = lax.dot_general(q.astype(BF16), kb16, (((1,), (1,)), ((), ())),
                                        preferred_element_type=F32) * decay
                o = _dot_bf16(q * jnp.exp(gc), s_old) + _dot_bf16(intra, v_new)
                o_ref = of_ref if d == 0 else ob_ref
                o_ref[:, h * DN_HEAD_DIM:(h + 1) * DN_HEAD_DIM] = o

    @pl.when(i == nc - 1)
    def _():
        sfin_ref[...] = s_ref[...]


def _gdn_scan(x, gates, s0, has_q):
    b, t, c = x.shape
    nc = t // CHUNK
    fwd = lambda bi, i: (bi, i, 0)
    bwd = lambda bi, i: (bi, nc - 1 - i, 0)
    state_spec = pl.BlockSpec((None, N_GATE, DN_HEAD_DIM, DN_HEAD_DIM), lambda bi, i: (bi, 0, 0, 0))
    state_shape = jax.ShapeDtypeStruct((b, N_GATE, DN_HEAD_DIM, DN_HEAD_DIM), F32)
    out_specs = [state_spec]
    out_shape = [state_shape]
    if has_q:
        o_shape = jax.ShapeDtypeStruct((b, t, DN_W), F32)
        out_specs = [pl.BlockSpec((None, CHUNK, DN_W), fwd), pl.BlockSpec((None, CHUNK, DN_W), bwd)] + out_specs
        out_shape = [o_shape, o_shape] + out_shape
    return pl.pallas_call(
        functools.partial(_gdn_kernel, has_q),
        grid=(b, nc),
        in_specs=[pl.BlockSpec((None, CHUNK, c), fwd), pl.BlockSpec((None, CHUNK, c), bwd),
                  pl.BlockSpec((None, CHUNK, GATE_W), fwd), pl.BlockSpec((None, CHUNK, GATE_W), bwd),
                  state_spec],
        out_specs=out_specs,
        out_shape=out_shape,
        scratch_shapes=[pltpu.VMEM((N_GATE, DN_HEAD_DIM, DN_HEAD_DIM), F32)],
        compiler_params=_cparams(("parallel", "arbitrary")),
        name="gdn_scan" if has_q else "gdn_scan_ctx",
    )(x, x, gates, gates, s0)


def _outproj_kernel(x_ref, oa_ref, of_ref, ob_ref, sz_ref, mod_ref, dnw_ref, w_ref, o_ref):
    o_dn = of_ref[...] + ob_ref[...]
    parts = [oa_ref[...]]
    for h in range(DN_HEADS):
        cols = slice(h * DN_HEAD_DIM, (h + 1) * DN_HEAD_DIM)
        seg = o_dn[:, cols]
        seg = seg * lax.rsqrt(jnp.mean(seg * seg, axis=-1, keepdims=True) + NORM_EPS) * dnw_ref[...]
        parts.append((seg * sz_ref[:, cols]).astype(BF16))
    mixed = jnp.dot(jnp.concatenate(parts, axis=1), w_ref[...], preferred_element_type=F32)
    o_ref[...] = x_ref[...] + mod_ref[2:3, :] * mixed


def _outproj(x, o_attn, o_f, o_b, sz, mod, dnw, w_out, tm):
    b, t, d = x.shape
    row = lambda bi, i: (bi, i, 0)
    full = lambda bi, i: (0, 0)
    return pl.pallas_call(
        _outproj_kernel,
        grid=(b, t // tm),
        in_specs=[pl.BlockSpec((None, tm, d), row),
                  pl.BlockSpec((None, tm, ATTN_W), row),
                  pl.BlockSpec((None, tm, DN_W), row),
                  pl.BlockSpec((None, tm, DN_W), row),
                  pl.BlockSpec((None, tm, DN_W), row),
                  pl.BlockSpec((None, N_MOD, d), lambda bi, i: (bi, 0, 0)),
                  pl.BlockSpec((1, DN_HEAD_DIM), full),
                  pl.BlockSpec(w_out.shape, full)],
        out_specs=pl.BlockSpec((None, tm, d), row),
        out_shape=jax.ShapeDtypeStruct((b, t, d), F32),
        compiler_params=_cparams(("parallel", "parallel")),
        name="outproj",
    )(x, o_attn, o_f, o_b, sz, mod, dnw, w_out)


def _mlp_kernel(ff_tile, x_ref, mod_ref, n2w_ref, w1_ref, w2_ref, o_ref):
    x = x_ref[...]
    y = x * lax.rsqrt(jnp.mean(x * x, axis=-1, keepdims=True) + NORM_EPS) * n2w_ref[...]
    h = (y * (1.0 + mod_ref[4:5, :]) + mod_ref[3:4, :]).astype(BF16)
    acc = None
    for c in range(w1_ref.shape[1] // ff_tile):
        a = jnp.dot(h, w1_ref[:, c * ff_tile:(c + 1) * ff_tile], preferred_element_type=F32)
        a = jnp.square(jnp.maximum(a, 0.0)).astype(BF16)
        part = jnp.dot(a, w2_ref[c * ff_tile:(c + 1) * ff_tile, :], preferred_element_type=F32)
        acc = part if acc is None else acc + part
    o_ref[...] = x + mod_ref[5:6, :] * acc


def _mlp(x, mod, n2w, w1, w2, tm, ff_tile):
    b, t, d = x.shape
    row = lambda bi, i: (bi, i, 0)
    full = lambda bi, i: (0, 0)
    return pl.pallas_call(
        functools.partial(_mlp_kernel, ff_tile),
        grid=(b, t // tm),
        in_specs=[pl.BlockSpec((None, tm, d), row),
                  pl.BlockSpec((None, N_MOD, d), lambda bi, i: (bi, 0, 0)),
                  pl.BlockSpec((1, d), full),
                  pl.BlockSpec(w1.shape, full, pipeline_mode=pl.Buffered(1)),
                  pl.BlockSpec(w2.shape, full, pipeline_mode=pl.Buffered(1))],
        out_specs=pl.BlockSpec((None, tm, d), row),
        out_shape=jax.ShapeDtypeStruct((b, t, d), F32),
        compiler_params=_cparams(("parallel", "parallel")),
        name="mlp",
    )(x, mod, n2w, w1, w2)


def _rope_tables(t):
    half = ATTN_HEAD_DIM // 4
    inv = ROPE_THETA ** (-jnp.arange(half, dtype=F32) / half)
    pos = jnp.arange(t, dtype=jnp.int32)
    ang_r = (pos // GRID_W).astype(F32)[:, None] * inv[None, :]
    ang_c = (pos % GRID_W).astype(F32)[:, None] * inv[None, :]
    cos = jnp.concatenate([jnp.cos(ang_r)] * 2 + [jnp.cos(ang_c)] * 2, axis=1)
    sin = jnp.concatenate([-jnp.sin(ang_r), jnp.sin(ang_r), -jnp.sin(ang_c), jnp.sin(ang_c)], axis=1)
    rep = LANES // ATTN_HEAD_DIM
    return jnp.tile(cos, (1, rep)), jnp.tile(sin, (1, rep))


def _pad_lanes(vec, offset):
    return jnp.zeros((1, GATE_W), F32).at[0, offset:offset + vec.size].set(vec.reshape(-1).astype(F32))


def kernel(x, c, ctx, c_ctx, w_mod, b_mod, norm1_w, w_in, q_norm_w, k_norm_w, conv_w, a_log, dt_bias,
           dn_norm_w, w_out, norm2_w, w_mlp1, w_mlp2):
    b, t, d = x.shape
    n_ctx = ctx.shape[1]
    depth = w_mod.shape[0]
    cos, sin = _rope_tables(t)
    head_id = jnp.arange(ATTN_W, dtype=jnp.int32) // ATTN_HEAD_DIM
    gmat = (head_id[:, None] == head_id[None, :]).astype(BF16)
    c_rows = jnp.zeros((8, d), F32).at[:b].set(c).at[b].set(c_ctx)
    tm = min(512, t)
    for l in range(depth):
        mod = _modulation(c_rows, w_mod[l], b_mod[l]).reshape(8, N_MOD, d)
        w_in_p = jnp.pad(w_in[l], ((0, 0), (0, IN_COLS_PAD - w_in.shape[2]))).astype(BF16)
        consts = (norm1_w[l].reshape(1, d), w_in_p,
                  jnp.tile(q_norm_w[l], ATTN_HEADS).reshape(1, ATTN_W),
                  jnp.tile(k_norm_w[l], ATTN_KV_HEADS).reshape(1, ATTN_KV_W),
                  cos, sin, gmat, _pad_lanes(a_log[l], N_GATE), _pad_lanes(dt_bias[l], N_GATE))
        q_a, k_a, v_a, dqkv, sz, gates = _inproj(x, mod, lambda bi, i: (bi, 0, 0), False, tm, consts)
        k_c, v_c, cdkv, cgates = _inproj(ctx, mod, lambda bi, i: (b, 0, 0), True, n_ctx, consts)

        o_attn = _attention(q_a, k_a, v_a, k_c, v_c, min(256, t), min(512, t))

        qkv = _short_conv_norm(dqkv, conv_w[l], 2 * DN_HEADS, tm)
        ckv = _short_conv_norm(cdkv, conv_w[l][:, DN_W:], DN_HEADS, n_ctx)
        s_zero = jnp.zeros((b, N_GATE, DN_HEAD_DIM, DN_HEAD_DIM), F32)
        (s_ctx,) = _gdn_scan(ckv, cgates, s_zero, False)
        o_f, o_b, _ = _gdn_scan(qkv, gates, s_ctx, True)

        x = _outproj(x, o_attn, o_f, o_b, sz, mod, dn_norm_w[l].reshape(1, DN_HEAD_DIM),
                     w_out[l].astype(BF16), tm)
        x = _mlp(x, mod, norm2_w[l].reshape(1, d), w_mlp1[l].astype(BF16), w_mlp2[l].astype(BF16),
                 tm, 1024)
    return x
```

```python
import functools
import math

import jax
import jax.numpy as jnp
from jax import lax
from jax.experimental import pallas as pl
from jax.experimental.pallas import tpu as pltpu

F32 = jnp.float32
BF16 = jnp.bfloat16

GRID_W = 64
ATTN_HEAD_DIM = 64
ATTN_HEADS = 8
ATTN_KV_HEADS = 2
ATTN_GROUP = ATTN_HEADS // ATTN_KV_HEADS
ATTN_W = ATTN_HEADS * ATTN_HEAD_DIM
ATTN_KV_W = ATTN_KV_HEADS * ATTN_HEAD_DIM
DN_HEAD_DIM = 128
DN_HEADS = 4
DN_W = DN_HEADS * DN_HEAD_DIM
N_DIR = 2
CONV_K = 5
CHUNK = 64
N_MOD = 6
ROPE_THETA = 10000.0
NORM_EPS = 1e-6
LANES = 128
GATE_W = LANES
N_GATE = N_DIR * DN_HEADS

OFF_Q = 0
OFF_K = OFF_Q + ATTN_W
OFF_V = OFF_K + ATTN_KV_W
OFF_DQKV = OFF_V + ATTN_KV_W
OFF_Z = OFF_DQKV + 3 * DN_W
OFF_GATE = OFF_Z + DN_W
IN_COLS_PAD = OFF_GATE + GATE_W

Q_EXT_W = ATTN_HEADS * LANES
V_EXT_W = 2 * LANES

VMEM_LIMIT = 56 * 1024 * 1024


def _cparams(sem):
    return pltpu.CompilerParams(dimension_semantics=sem, vmem_limit_bytes=VMEM_LIMIT)


def _dot_bf16(a, b):
    return jnp.dot(a.astype(BF16), b.astype(BF16), preferred_element_type=F32)


def _dot_f32(a, b):
    return jnp.dot(a, b, preferred_element_type=F32, precision=lax.Precision.HIGHEST)


def _split_bf16(a):
    hi = a.astype(BF16)
    lo = (a - hi.astype(F32)).astype(BF16)
    return hi, lo


def _dot_x3(a, b):
    ah, al = _split_bf16(a)
    bh, bl = _split_bf16(b)
    d = functools.partial(jnp.dot, preferred_element_type=F32)
    return d(ah, bh) + (d(ah, bl) + d(al, bh))


def _silu(x):
    return x * jax.nn.sigmoid(x)


def _mod_kernel(c_ref, w_ref, b_ref, o_ref):
    o_ref[...] = _dot_f32(_silu(c_ref[...]), w_ref[...]) + b_ref[...]


def _modulation(c_rows, w_mod, b_mod):
    r, d = c_rows.shape
    n = w_mod.shape[1]
    tn = 1024
    return pl.pallas_call(
        _mod_kernel,
        grid=(n // tn,),
        in_specs=[pl.BlockSpec((r, d), lambda j: (0, 0)),
                  pl.BlockSpec((d, tn), lambda j: (0, j)),
                  pl.BlockSpec((1, tn), lambda j: (0, j))],
        out_specs=pl.BlockSpec((r, tn), lambda j: (0, j)),
        out_shape=jax.ShapeDtypeStruct((r, n), F32),
        compiler_params=_cparams(("arbitrary",)),
        name="modulation",
    )(c_rows, w_mod, b_mod.reshape(1, n))


def _group_mean_sq(a, gmat, group):
    hi, lo = _split_bf16(a * a)
    d = functools.partial(jnp.dot, preferred_element_type=F32)
    return (d(hi, gmat) + d(lo, gmat)) * (1.0 / group)


def _rope(a, cos, sin):
    parts = []
    for c in range(a.shape[1] // LANES):
        ac = a[:, c * LANES:(c + 1) * LANES]
        lane = lax.broadcasted_iota(jnp.int32, ac.shape, 1)
        nxt = pltpu.roll(ac, LANES - 16, 1)
        prv = pltpu.roll(ac, 16, 1)
        parts.append(jnp.where((lane & 31) < 16, nxt, prv))
    partner = parts[0] if len(parts) == 1 else jnp.concatenate(parts, axis=1)
    return a * cos + partner * sin


def _gate_maps(raw, alog, dtb):
    lane = lax.broadcasted_iota(jnp.int32, raw.shape, 1)
    beta = jax.nn.sigmoid(raw)
    y = raw + dtb
    softplus = jnp.maximum(y, 0.0) + jnp.log1p(jnp.exp(-jnp.abs(y)))
    g = -jnp.exp(alog) * softplus
    return jnp.where(lane < N_GATE, beta, g)


def _inproj_kernel(is_ctx, x_ref, mod_ref, n1w_ref, w_ref, qnw_ref, knw_ref, cos_ref, sin_ref,
                   gmat_ref, alog_ref, dtb_ref, *outs):
    x = x_ref[...]
    y = x * lax.rsqrt(jnp.mean(x * x, axis=-1, keepdims=True) + NORM_EPS) * n1w_ref[...]
    h = (y * (1.0 + mod_ref[1:2, :]) + mod_ref[0:1, :]).astype(BF16)

    def proj(lo, hi):
        return jnp.dot(h, w_ref[:, lo:hi], preferred_element_type=F32)

    kk = proj(OFF_K, OFF_V)
    kk = kk * lax.rsqrt(_group_mean_sq(kk, gmat_ref[:ATTN_KV_W, :ATTN_KV_W], ATTN_HEAD_DIM)
                        + NORM_EPS) * knw_ref[...]
    vv = proj(OFF_V, OFF_DQKV)
    gates = _gate_maps(proj(OFF_GATE, IN_COLS_PAD), alog_ref[...], dtb_ref[...])
    if is_ctx:
        k_out, v_out, dkv_out, gate_out = outs
        dkv_out[...] = proj(OFF_DQKV + DN_W, OFF_Z)
    else:
        q_out, k_out, v_out, dqkv_out, sz_out, gate_out = outs
        cos = cos_ref[...]
        sin = sin_ref[...]
        kk = _rope(kk, cos, sin)
        qq = proj(OFF_Q, OFF_K)
        qq = qq * lax.rsqrt(_group_mean_sq(qq, gmat_ref[...], ATTN_HEAD_DIM) + NORM_EPS) * qnw_ref[...]
        rep = ATTN_W // LANES
        qq = _rope(qq, jnp.concatenate([cos] * rep, axis=1), jnp.concatenate([sin] * rep, axis=1))
        qb = (qq * (ATTN_HEAD_DIM ** -0.5)).astype(q_out.dtype)
        zeros = jnp.zeros((qb.shape[0], ATTN_HEAD_DIM), q_out.dtype)
        for hd in range(ATTN_HEADS):
            seg = qb[:, hd * ATTN_HEAD_DIM:(hd + 1) * ATTN_HEAD_DIM]
            pair = [seg, zeros] if hd // ATTN_GROUP == 0 else [zeros, seg]
            q_out[:, hd * LANES:(hd + 1) * LANES] = jnp.concatenate(pair, axis=1)
        dqkv_out[...] = proj(OFF_DQKV, OFF_Z)
        sz_out[...] = _silu(proj(OFF_Z, OFF_GATE))
    k_out[...] = kk.astype(k_out.dtype)
    v_out[:, :ATTN_KV_W] = vv.astype(v_out.dtype)
    v_out[:, ATTN_KV_W:] = jnp.ones((vv.shape[0], V_EXT_W - ATTN_KV_W), v_out.dtype)
    gate_out[...] = gates


def _inproj(x, mod, mod_index, is_ctx, tm, consts):
    b, t, d = x.shape
    n1w, w_in, qnw, knw, cos, sin, gmat, alog, dtb = consts
    nt = t // tm
    row = lambda bi, i: (bi, i, 0)
    full = lambda bi, i: (0, 0)
    in_specs = [
        pl.BlockSpec((None, tm, d), row),
        pl.BlockSpec((None, N_MOD, d), mod_index),
        pl.BlockSpec((1, d), full),
        pl.BlockSpec(w_in.shape, full),
        pl.BlockSpec(qnw.shape, full),
        pl.BlockSpec(knw.shape, full),
        pl.BlockSpec((tm, LANES), lambda bi, i: (i, 0)),
        pl.BlockSpec((tm, LANES), lambda bi, i: (i, 0)),
        pl.BlockSpec(gmat.shape, full),
        pl.BlockSpec((1, GATE_W), full),
        pl.BlockSpec((1, GATE_W), full),
    ]

    def out(width, dtype):
        return pl.BlockSpec((None, tm, width), row), jax.ShapeDtypeStruct((b, t, width), dtype)

    if is_ctx:
        outs = [out(ATTN_KV_W, BF16), out(V_EXT_W, BF16), out(2 * DN_W, F32), out(GATE_W, F32)]
    else:
        outs = [out(Q_EXT_W, BF16), out(ATTN_KV_W, BF16), out(V_EXT_W, BF16),
                out(3 * DN_W, F32), out(DN_W, F32), out(GATE_W, F32)]
    return pl.pallas_call(
        functools.partial(_inproj_kernel, is_ctx),
        grid=(b, nt),
        in_specs=in_specs,
        out_specs=[o[0] for o in outs],
        out_shape=[o[1] for o in outs],
        compiler_params=_cparams(("parallel", "parallel")),
        name="inproj_ctx" if is_ctx else "inproj",
    )(x, mod, n1w, w_in, qnw, knw, cos, sin, gmat, alog, dtb)


HALO = 8


def _conv_kernel(n_norm, x_ref, prev_ref, next_ref, w_ref, o_ref, win_ref):
    i = pl.program_id(1)
    nt = pl.num_programs(1)
    tc = x_ref.shape[0]
    win_ref[0:HALO, :] = jnp.where(i > 0, prev_ref[...], 0.0)
    win_ref[HALO:HALO + tc, :] = x_ref[...]
    win_ref[HALO + tc:, :] = jnp.where(i < nt - 1, next_ref[...], 0.0)
    for h in range(x_ref.shape[1] // LANES):
        cols = slice(h * LANES, (h + 1) * LANES)
        acc = None
        for j in range(CONV_K):
            start = HALO - CONV_K // 2 + j
            term = win_ref[start:start + tc, cols] * w_ref[j:j + 1, cols]
            acc = term if acc is None else acc + term
        y = _silu(acc)
        if h < n_norm:
            y = y * lax.rsqrt(jnp.sum(y * y, axis=-1, keepdims=True) + NORM_EPS)
        o_ref[:, cols] = y


def _short_conv_norm(x, w, n_norm, tc):
    b, t, c = x.shape
    nt = t // tc
    per = tc // HALO
    last = t // HALO - 1
    return pl.pallas_call(
        functools.partial(_conv_kernel, n_norm),
        grid=(b, nt),
        in_specs=[
            pl.BlockSpec((None, tc, c), lambda bi, i: (bi, i, 0)),
            pl.BlockSpec((None, HALO, c), lambda bi, i: (bi, jnp.maximum(i * per - 1, 0), 0)),
            pl.BlockSpec((None, HALO, c), lambda bi, i: (bi, jnp.minimum((i + 1) * per, last), 0)),
            pl.BlockSpec(w.shape, lambda bi, i: (0, 0)),
        ],
        out_specs=pl.BlockSpec((None, tc, c), lambda bi, i: (bi, i, 0)),
        out_shape=jax.ShapeDtypeStruct((b, t, c), F32),
        scratch_shapes=[pltpu.VMEM((tc + 2 * HALO, c), F32)],
        compiler_params=_cparams(("parallel", "parallel")),
        name="short_conv",
    )(x, x, x, w)


def _attn_kernel(q_ref, kc_ref, vc_ref, k_ref, v_ref, o_ref, m_ref, acc_ref):
    ki = pl.program_id(2)
    nk = pl.num_programs(2)

    def process(kb, vb):
        reps = kb.shape[0] // LANES
        for h in range(ATTN_HEADS):
            qh = q_ref[:, h * LANES:(h + 1) * LANES]
            s = lax.dot_general(qh, kb, (((1,), (1,)), ((), ())), preferred_element_type=F32)
            m_prev = m_ref[h]
            m_new = jnp.maximum(m_prev, jnp.max(s, axis=-1, keepdims=True))
            alpha = jnp.exp(m_prev - m_new)
            p = jnp.exp(s - jnp.concatenate([m_new] * reps, axis=1))
            pv = jnp.dot(p.astype(BF16), vb, preferred_element_type=F32)
            acc_ref[h] = jnp.concatenate([alpha] * (V_EXT_W // LANES), axis=1) * acc_ref[h] + pv
            m_ref[h] = m_new

    @pl.when(ki == 0)
    def _():
        m_ref[...] = jnp.full(m_ref.shape, -jnp.inf, F32)
        acc_ref[...] = jnp.zeros(acc_ref.shape, F32)
        process(kc_ref[...], vc_ref[...])

    process(k_ref[...], v_ref[...])

    @pl.when(ki == nk - 1)
    def _():
        for h in range(ATTN_HEADS):
            j = h // ATTN_GROUP
            acc = acc_ref[h]
            num = acc[:, j * ATTN_HEAD_DIM:(j + 1) * ATTN_HEAD_DIM]
            den = acc[:, ATTN_KV_W + j * ATTN_HEAD_DIM:ATTN_KV_W + (j + 1) * ATTN_HEAD_DIM]
            o_ref[:, h * ATTN_HEAD_DIM:(h + 1) * ATTN_HEAD_DIM] = (num / den).astype(o_ref.dtype)


def _attention(q, k, v, kc, vc, tq, tk):
    b, t, _ = q.shape
    n_ctx = kc.shape[1]
    return pl.pallas_call(
        _attn_kernel,
        grid=(b, t // tq, t // tk),
        in_specs=[
            pl.BlockSpec((None, tq, Q_EXT_W), lambda bi, qi, ki: (bi, qi, 0)),
            pl.BlockSpec((None, n_ctx, ATTN_KV_W), lambda bi, qi, ki: (bi, 0, 0)),
            pl.BlockSpec((None, n_ctx, V_EXT_W), lambda bi, qi, ki: (bi, 0, 0)),
            pl.BlockSpec((None, tk, ATTN_KV_W), lambda bi, qi, ki: (bi, ki, 0)),
            pl.BlockSpec((None, tk, V_EXT_W), lambda bi, qi, ki: (bi, ki, 0)),
        ],
        out_specs=pl.BlockSpec((None, tq, ATTN_W), lambda bi, qi, ki: (bi, qi, 0)),
        out_shape=jax.ShapeDtypeStruct((b, t, ATTN_W), BF16),
        scratch_shapes=[pltpu.VMEM((ATTN_HEADS, tq, LANES), F32),
                        pltpu.VMEM((ATTN_HEADS, tq, V_EXT_W), F32)],
        compiler_params=_cparams(("parallel", "parallel", "arbitrary")),
        name="attention",
    )(q, kc, vc, k, v)


def _unit_tri_inverse(n_strict, row, col):
    eye = (row == col).astype(F32)
    d = eye
    s = 1
    while s < CHUNK:
        sh = s.bit_length() - 1
        join = ((row >> (sh + 1)) == (col >> (sh + 1))) & ((row >> sh) != (col >> sh))
        e = jnp.where(join, n_strict, 0.0)
        if s == 1:
            d = eye - e
        else:
            d = d - _dot_x3(d, _dot_x3(e, d))
        s *= 2
    return d


def _gdn_kernel(has_q, *refs):
    if has_q:
        (xf_ref, xb_ref, gf_ref, gb_ref, s0_ref, of_ref, ob_ref, sfin_ref, s_ref) = refs
    else:
        (xf_ref, xb_ref, gf_ref, gb_ref, s0_ref, sfin_ref, s_ref) = refs
    i = pl.program_id(1)
    nc = pl.num_programs(1)

    @pl.when(i == 0)
    def _():
        s_ref[...] = s0_ref[...]

    row = lax.broadcasted_iota(jnp.int32, (CHUNK, CHUNK), 0)
    col = lax.broadcasted_iota(jnp.int32, (CHUNK, CHUNK), 1)
    off_k = DN_W if has_q else 0
    off_v = off_k + DN_W
    for d in range(N_DIR):
        x_ref, g_ref = (xf_ref, gf_ref) if d == 0 else (xb_ref, gb_ref)
        incl = (row >= col) if d == 0 else (row <= col)
        incl_t = (col >= row) if d == 0 else (col <= row)
        strict = (row > col) if d == 0 else (row < col)
        last = CHUNK - 1 if d == 0 else 0
        gates = g_ref[...]
        csum = _dot_f32(incl.astype(F32), gates)
        csum_t = _dot_f32(gates.T, incl_t.astype(F32))
        for h in range(DN_HEADS):
            u_idx = d * DN_HEADS + h
            beta = gates[:, u_idx:u_idx + 1]
            gc = csum[:, N_GATE + u_idx:N_GATE + u_idx + 1]
            gc_t = csum_t[N_GATE + u_idx:N_GATE + u_idx + 1, :]
            g_last = gc[last:last + 1, :]
            k = x_ref[:, off_k + h * DN_HEAD_DIM:off_k + (h + 1) * DN_HEAD_DIM]
            v = x_ref[:, off_v + h * DN_HEAD_DIM:off_v + (h + 1) * DN_HEAD_DIM]
            decay = jnp.exp(jnp.where(incl, gc - gc_t, -jnp.inf))
            kb = k * beta
            kb16 = k.astype(BF16)
            kk = lax.dot_general(kb.astype(BF16), kb16, (((1,), (1,)), ((), ())),
                                 preferred_element_type=F32)
            t_inv = _unit_tri_inverse(jnp.where(strict, kk * decay, 0.0), row, col)
            rhs = jnp.concatenate([v * beta, kb * jnp.exp(gc)], axis=1)
            uw = _dot_bf16(t_inv, rhs)
            u = uw[:, :DN_HEAD_DIM]
            w = uw[:, DN_HEAD_DIM:]
            k_dec = k * jnp.exp(g_last - gc)
            s_old = s_ref[u_idx]
            v_new = u - _dot_bf16(w, s_old)
            s_ref[u_idx] = s_old * jnp.exp(g_last) + lax.dot_general(
                k_dec.astype(BF16), v_new.astype(BF16), (((0,), (0,)), ((), ())),
                preferred_element_type=F32)
            if has_q:
                q = x_ref[:, h * DN_HEAD_DIM:(h + 1) * DN_HEAD_DIM] * (DN_HEAD_DIM ** -0.5)
                intra = lax.dot_general(q.astype(BF16), kb16, (((1,), (1,)), ((), ())),
                                        preferred_element_type=F32) * decay
                o = _dot_bf16(q * jnp.exp(gc), s_old) + _dot_bf16(intra, v_new)
                o_ref = of_ref if d == 0 else ob_ref
                o_ref[:, h * DN_HEAD_DIM:(h + 1) * DN_HEAD_DIM] = o

    @pl.when(i == nc - 1)
    def _():
        sfin_ref[...] = s_ref[...]


def _gdn_scan(x, gates, s0, has_q):
    b, t, c = x.shape
    nc = t // CHUNK
    fwd = lambda bi, i: (bi, i, 0)
    bwd = lambda bi, i: (bi, nc - 1 - i, 0)
    state_spec = pl.BlockSpec((None, N_GATE, DN_HEAD_DIM, DN_HEAD_DIM), lambda bi, i: (bi, 0, 0, 0))
    state_shape = jax.ShapeDtypeStruct((b, N_GATE, DN_HEAD_DIM, DN_HEAD_DIM), F32)
    out_specs = [state_spec]
    out_shape = [state_shape]
    if has_q:
        o_shape = jax.ShapeDtypeStruct((b, t, DN_W), F32)
        out_specs = [pl.BlockSpec((None, CHUNK, DN_W), fwd), pl.BlockSpec((None, CHUNK, DN_W), bwd)] + out_specs
        out_shape = [o_shape, o_shape] + out_shape
    return pl.pallas_call(
        functools.partial(_gdn_kernel, has_q),
        grid=(b, nc),
        in_specs=[pl.BlockSpec((None, CHUNK, c), fwd), pl.BlockSpec((None, CHUNK, c), bwd),
                  pl.BlockSpec((None, CHUNK, GATE_W), fwd), pl.BlockSpec((None, CHUNK, GATE_W), bwd),
                  state_spec],
        out_specs=out_specs,
        out_shape=out_shape,
        scratch_shapes=[pltpu.VMEM((N_GATE, DN_HEAD_DIM, DN_HEAD_DIM), F32)],
        compiler_params=_cparams(("parallel", "arbitrary")),
        name="gdn_scan" if has_q else "gdn_scan_ctx",
    )(x, x, gates, gates, s0)


def _outproj_kernel(x_ref, oa_ref, of_ref, ob_ref, sz_ref, mod_ref, dnw_ref, w_ref, o_ref):
    o_dn = of_ref[...] + ob_ref[...]
    parts = [oa_ref[...]]
    for h in range(DN_HEADS):
        cols = slice(h * DN_HEAD_DIM, (h + 1) * DN_HEAD_DIM)
        seg = o_dn[:, cols]
        seg = seg * lax.rsqrt(jnp.mean(seg * seg, axis=-1, keepdims=True) + NORM_EPS) * dnw_ref[...]
        parts.append((seg * sz_ref[:, cols]).astype(BF16))
    mixed = jnp.dot(jnp.concatenate(parts, axis=1), w_ref[...], preferred_element_type=F32)
    o_ref[...] = x_ref[...] + mod_ref[2:3, :] * mixed


def _outproj(x, o_attn, o_f, o_b, sz, mod, dnw, w_out, tm):
    b, t, d = x.shape
    row = lambda bi, i: (bi, i, 0)
    full = lambda bi, i: (0, 0)
    return pl.pallas_call(
        _outproj_kernel,
        grid=(b, t // tm),
        in_specs=[pl.BlockSpec((None, tm, d), row),
                  pl.BlockSpec((None, tm, ATTN_W), row),
                  pl.BlockSpec((None, tm, DN_W), row),
                  pl.BlockSpec((None, tm, DN_W), row),
                  pl.BlockSpec((None, tm, DN_W), row),
                  pl.BlockSpec((None, N_MOD, d), lambda bi, i: (bi, 0, 0)),
                  pl.BlockSpec((1, DN_HEAD_DIM), full),
                  pl.BlockSpec(w_out.shape, full)],
        out_specs=pl.BlockSpec((None, tm, d), row),
        out_shape=jax.ShapeDtypeStruct((b, t, d), F32),
        compiler_params=_cparams(("parallel", "parallel")),
        name="outproj",
    )(x, o_attn, o_f, o_b, sz, mod, dnw, w_out)


def _mlp_kernel(ff_tile, x_ref, mod_ref, n2w_ref, w1_ref, w2_ref, o_ref):
    x = x_ref[...]
    y = x * lax.rsqrt(jnp.mean(x * x, axis=-1, keepdims=True) + NORM_EPS) * n2w_ref[...]
    h = (y * (1.0 + mod_ref[4:5, :]) + mod_ref[3:4, :]).astype(BF16)
    acc = None
    for c in range(w1_ref.shape[1] // ff_tile):
        a = jnp.dot(h, w1_ref[:, c * ff_tile:(c + 1) * ff_tile], preferred_element_type=F32)
        a = jnp.square(jnp.maximum(a, 0.0)).astype(BF16)
        part = jnp.dot(a, w2_ref[c * ff_tile:(c + 1) * ff_tile, :], preferred_element_type=F32)
        acc = part if acc is None else acc + part
    o_ref[...] = x + mod_ref[5:6, :] * acc


def _mlp(x, mod, n2w, w1, w2, tm, ff_tile):
    b, t, d = x.shape
    row = lambda bi, i: (bi, i, 0)
    full = lambda bi, i: (0, 0)
    return pl.pallas_call(
        functools.partial(_mlp_kernel, ff_tile),
        grid=(b, t // tm),
        in_specs=[pl.BlockSpec((None, tm, d), row),
                  pl.BlockSpec((None, N_MOD, d), lambda bi, i: (bi, 0, 0)),
                  pl.BlockSpec((1, d), full),
                  pl.BlockSpec(w1.shape, full, pipeline_mode=pl.Buffered(1)),
                  pl.BlockSpec(w2.shape, full, pipeline_mode=pl.Buffered(1))],
        out_specs=pl.BlockSpec((None, tm, d), row),
        out_shape=jax.ShapeDtypeStruct((b, t, d), F32),
        compiler_params=_cparams(("parallel", "parallel")),
        name="mlp",
    )(x, mod, n2w, w1, w2)


def _rope_tables(t):
    half = ATTN_HEAD_DIM // 4
    inv = ROPE_THETA ** (-jnp.arange(half, dtype=F32) / half)
    pos = jnp.arange(t, dtype=jnp.int32)
    ang_r = (pos // GRID_W).astype(F32)[:, None] * inv[None, :]
    ang_c = (pos % GRID_W).astype(F32)[:, None] * inv[None, :]
    cos = jnp.concatenate([jnp.cos(ang_r)] * 2 + [jnp.cos(ang_c)] * 2, axis=1)
    sin = jnp.concatenate([-jnp.sin(ang_r), jnp.sin(ang_r), -jnp.sin(ang_c), jnp.sin(ang_c)], axis=1)
    rep = LANES // ATTN_HEAD_DIM
    return jnp.tile(cos, (1, rep)), jnp.tile(sin, (1, rep))


def _pad_lanes(vec, offset):
    return jnp.zeros((1, GATE_W), F32).at[0, offset:offset + vec.size].set(vec.reshape(-1).astype(F32))


def kernel(x, c, ctx, c_ctx, w_mod, b_mod, norm1_w, w_in, q_norm_w, k_norm_w, conv_w, a_log, dt_bias,
           dn_norm_w, w_out, norm2_w, w_mlp1, w_mlp2):
    b, t, d = x.shape
    n_ctx = ctx.shape[1]
    depth = w_mod.shape[0]
    cos, sin = _rope_tables(t)
    head_id = jnp.arange(ATTN_W, dtype=jnp.int32) // ATTN_HEAD_DIM
    gmat = (head_id[:, None] == head_id[None, :]).astype(BF16)
    c_rows = jnp.zeros((8, d), F32).at[:b].set(c).at[b].set(c_ctx)
    tm = min(512, t)
    for l in range(depth):
        mod = _modulation(c_rows, w_mod[l], b_mod[l]).reshape(8, N_MOD, d)
        w_in_p = jnp.pad(w_in[l], ((0, 0), (0, IN_COLS_PAD - w_in.shape[2]))).astype(BF16)
        consts = (norm1_w[l].reshape(1, d), w_in_p,
                  jnp.tile(q_norm_w[l], ATTN_HEADS).reshape(1, ATTN_W),
                  jnp.tile(k_norm_w[l], ATTN_KV_HEADS).reshape(1, ATTN_KV_W),
                  cos, sin, gmat, _pad_lanes(a_log[l], N_GATE), _pad_lanes(dt_bias[l], N_GATE))
        q_a, k_a, v_a, dqkv, sz, gates = _inproj(x, mod, lambda bi, i: (bi, 0, 0), False, tm, consts)
        k_c, v_c, cdkv, cgates = _inproj(ctx, mod, lambda bi, i: (b, 0, 0), True, n_ctx, consts)

        o_attn = _attention(q_a, k_a, v_a, k_c, v_c, min(512, t), min(1024, t))

        qkv = _short_conv_norm(dqkv, conv_w[l], 2 * DN_HEADS, tm)
        ckv = _short_conv_norm(cdkv, conv_w[l][:, DN_W:], DN_HEADS, n_ctx)
        s_zero = jnp.zeros((b, N_GATE, DN_HEAD_DIM, DN_HEAD_DIM), F32)
        (s_ctx,) = _gdn_scan(ckv, cgates, s_zero, False)
        o_f, o_b, _ = _gdn_scan(qkv, gates, s_ctx, True)

        x = _outproj(x, o_attn, o_f, o_b, sz, mod, dn_norm_w[l].reshape(1, DN_HEAD_DIM),
                     w_out[l].astype(BF16), tm)
        x = _mlp(x, mod, norm2_w[l].reshape(1, d), w_mlp1[l].astype(BF16), w_mlp2[l].astype(BF16),
                 tm, 1024)
    return x
```

```python
import functools

import jax
import jax.numpy as jnp
from jax import lax
from jax.experimental import pallas as pl
from jax.experimental.pallas import tpu as pltpu

F32 = jnp.float32
BF16 = jnp.bfloat16

GRID_W = 64
ATTN_HEAD_DIM = 64
ATTN_HEADS = 8
ATTN_KV_HEADS = 2
ATTN_GROUP = ATTN_HEADS // ATTN_KV_HEADS
ATTN_W = ATTN_HEADS * ATTN_HEAD_DIM
ATTN_KV_W = ATTN_KV_HEADS * ATTN_HEAD_DIM
DN_HEAD_DIM = 128
DN_HEADS = 4
DN_W = DN_HEADS * DN_HEAD_DIM
N_DIR = 2
CONV_K = 5
CHUNK = 64
N_MOD = 6
ROPE_THETA = 10000.0
NORM_EPS = 1e-6
LANES = 128
GATE_W = LANES
N_GATE = N_DIR * DN_HEADS

OFF_Q = 0
OFF_K = OFF_Q + ATTN_W
OFF_V = OFF_K + ATTN_KV_W
OFF_DQKV = OFF_V + ATTN_KV_W
OFF_Z = OFF_DQKV + 3 * DN_W
OFF_GATE = OFF_Z + DN_W
IN_COLS_PAD = OFF_GATE + GATE_W

Q_EXT_W = ATTN_HEADS * LANES
V_EXT_W = 2 * LANES

VMEM_LIMIT = 56 * 1024 * 1024


def _cparams(sem):
    return pltpu.CompilerParams(dimension_semantics=sem, vmem_limit_bytes=VMEM_LIMIT)


def _dot_bf16(a, b):
    return jnp.dot(a.astype(BF16), b.astype(BF16), preferred_element_type=F32)


def _dot_f32(a, b):
    return jnp.dot(a, b, preferred_element_type=F32, precision=lax.Precision.HIGHEST)


def _split_bf16(a):
    hi = a.astype(BF16)
    lo = (a - hi.astype(F32)).astype(BF16)
    return hi, lo


def _silu(x):
    return x * jax.nn.sigmoid(x)


def _mod_kernel(c_ref, w_ref, b_ref, o_ref):
    o_ref[...] = _dot_f32(_silu(c_ref[...]), w_ref[...]) + b_ref[...]


def _modulation(c_rows, w_mod, b_mod):
    r, d = c_rows.shape
    n = w_mod.shape[1]
    tn = 1024
    return pl.pallas_call(
        _mod_kernel,
        grid=(n // tn,),
        in_specs=[pl.BlockSpec((r, d), lambda j: (0, 0)),
                  pl.BlockSpec((d, tn), lambda j: (0, j)),
                  pl.BlockSpec((1, tn), lambda j: (0, j))],
        out_specs=pl.BlockSpec((r, tn), lambda j: (0, j)),
        out_shape=jax.ShapeDtypeStruct((r, n), F32),
        compiler_params=_cparams(("arbitrary",)),
        name="modulation",
    )(c_rows, w_mod, b_mod.reshape(1, n))


def _group_mean_sq(a, gmat, group):
    hi, lo = _split_bf16(a * a)
    d = functools.partial(jnp.dot, preferred_element_type=F32)
    return (d(hi, gmat) + d(lo, gmat)) * (1.0 / group)


def _rope(a, cos, sin):
    parts = []
    for c in range(a.shape[1] // LANES):
        ac = a[:, c * LANES:(c + 1) * LANES]
        lane = lax.broadcasted_iota(jnp.int32, ac.shape, 1)
        nxt = pltpu.roll(ac, LANES - 16, 1)
        prv = pltpu.roll(ac, 16, 1)
        parts.append(jnp.where((lane & 31) < 16, nxt, prv))
    partner = parts[0] if len(parts) == 1 else jnp.concatenate(parts, axis=1)
    return a * cos + partner * sin


def _chunk_scans(x):
    n = x.shape[0]
    pos = lax.broadcasted_iota(jnp.int32, x.shape, 0) & (CHUNK - 1)
    fwd, rev = x, x
    s = 1
    while s < CHUNK:
        fwd = fwd + jnp.where(pos >= s, pltpu.roll(fwd, s, 0), 0.0)
        rev = rev + jnp.where(pos + s < CHUNK, pltpu.roll(rev, n - s, 0), 0.0)
        s *= 2
    return fwd, rev


def _gate_maps(raw, alog, dtb):
    lane = lax.broadcasted_iota(jnp.int32, raw.shape, 1)
    beta = jax.nn.sigmoid(raw)
    y = raw + dtb
    softplus = jnp.maximum(y, 0.0) + jnp.log1p(jnp.exp(-jnp.abs(y)))
    g = -jnp.exp(alog) * softplus
    fwd, rev = _chunk_scans(g)
    gc = jnp.where(lane < N_GATE + DN_HEADS, fwd, rev)
    total = fwd + rev - g
    e_gc = pltpu.roll(jnp.exp(gc), N_GATE, 1)
    e_dec = pltpu.roll(jnp.exp(total - gc), 2 * N_GATE, 1)
    e_tot = pltpu.roll(jnp.exp(total), 3 * N_GATE, 1)
    out = jnp.where(lane < N_GATE, beta,
                    jnp.where(lane < 2 * N_GATE, gc,
                              jnp.where(lane < 3 * N_GATE, e_gc,
                                        jnp.where(lane < 4 * N_GATE, e_dec, e_tot))))
    return out, gc


def _inproj_kernel(is_ctx, x_ref, mod_ref, n1w_ref, w_ref, qnw_ref, knw_ref, cos_ref, sin_ref,
                   gmat_ref, alog_ref, dtb_ref, *outs):
    x = x_ref[...]
    y = x * lax.rsqrt(jnp.mean(x * x, axis=-1, keepdims=True) + NORM_EPS) * n1w_ref[...]
    h = (y * (1.0 + mod_ref[1:2, :]) + mod_ref[0:1, :]).astype(BF16)

    def proj(lo, hi):
        return jnp.dot(h, w_ref[:, lo:hi], preferred_element_type=F32)

    kk = proj(OFF_K, OFF_V)
    kk = kk * lax.rsqrt(_group_mean_sq(kk, gmat_ref[:ATTN_KV_W, :ATTN_KV_W], ATTN_HEAD_DIM)
                        + NORM_EPS) * knw_ref[...]
    vv = proj(OFF_V, OFF_DQKV)
    gates, gc = _gate_maps(proj(OFF_GATE, IN_COLS_PAD), alog_ref[...], dtb_ref[...])
    if is_ctx:
        k_out, v_out, dkv_out, gate_out, gt_out = outs
        dkv_out[...] = proj(OFF_DQKV + DN_W, OFF_Z)
    else:
        q_out, k_out, v_out, dqkv_out, sz_out, gate_out, gt_out = outs
        cos = cos_ref[...]
        sin = sin_ref[...]
        kk = _rope(kk, cos, sin)
        qq = proj(OFF_Q, OFF_K)
        qq = qq * lax.rsqrt(_group_mean_sq(qq, gmat_ref[...], ATTN_HEAD_DIM) + NORM_EPS) * qnw_ref[...]
        rep = ATTN_W // LANES
        qq = _rope(qq, jnp.concatenate([cos] * rep, axis=1), jnp.concatenate([sin] * rep, axis=1))
        qb = (qq * (ATTN_HEAD_DIM ** -0.5)).astype(q_out.dtype)
        zeros = jnp.zeros((qb.shape[0], ATTN_HEAD_DIM), q_out.dtype)
        for hd in range(ATTN_HEADS):
            seg = qb[:, hd * ATTN_HEAD_DIM:(hd + 1) * ATTN_HEAD_DIM]
            pair = [seg, zeros] if hd // ATTN_GROUP == 0 else [zeros, seg]
            q_out[:, hd * LANES:(hd + 1) * LANES] = jnp.concatenate(pair, axis=1)
        dqkv_out[...] = proj(OFF_DQKV, OFF_Z)
        sz_out[...] = _silu(proj(OFF_Z, OFF_GATE))
    k_out[...] = kk.astype(k_out.dtype)
    v_out[:, :ATTN_KV_W] = vv.astype(v_out.dtype)
    v_out[:, ATTN_KV_W:] = jnp.ones((vv.shape[0], V_EXT_W - ATTN_KV_W), v_out.dtype)
    gate_out[...] = gates
    gc_t = gc.T
    for ch in range(gt_out.shape[0]):
        gt_out[ch] = gc_t[N_GATE:2 * N_GATE, ch * CHUNK:(ch + 1) * CHUNK]


def _inproj(x, mod, mod_index, is_ctx, tm, consts):
    b, t, d = x.shape
    n1w, w_in, qnw, knw, cos, sin, gmat, alog, dtb = consts
    nt = t // tm
    row = lambda bi, i: (bi, i, 0)
    full = lambda bi, i: (0, 0)
    in_specs = [
        pl.BlockSpec((None, tm, d), row),
        pl.BlockSpec((None, N_MOD, d), mod_index),
        pl.BlockSpec((1, d), full),
        pl.BlockSpec(w_in.shape, full),
        pl.BlockSpec(qnw.shape, full),
        pl.BlockSpec(knw.shape, full),
        pl.BlockSpec((tm, LANES), lambda bi, i: (i, 0)),
        pl.BlockSpec((tm, LANES), lambda bi, i: (i, 0)),
        pl.BlockSpec(gmat.shape, full),
        pl.BlockSpec((1, GATE_W), full),
        pl.BlockSpec((1, GATE_W), full),
    ]

    def out(width, dtype):
        return pl.BlockSpec((None, tm, width), row), jax.ShapeDtypeStruct((b, t, width), dtype)

    if is_ctx:
        outs = [out(ATTN_KV_W, BF16), out(V_EXT_W, BF16), out(2 * DN_W, F32), out(GATE_W, F32)]
    else:
        outs = [out(Q_EXT_W, BF16), out(ATTN_KV_W, BF16), out(V_EXT_W, BF16),
                out(3 * DN_W, F32), out(DN_W, F32), out(GATE_W, F32)]
    outs.append((pl.BlockSpec((None, tm // CHUNK, N_GATE, CHUNK), lambda bi, i: (bi, i, 0, 0)),
                 jax.ShapeDtypeStruct((b, t // CHUNK, N_GATE, CHUNK), F32)))
    return pl.pallas_call(
        functools.partial(_inproj_kernel, is_ctx),
        grid=(b, nt),
        in_specs=in_specs,
        out_specs=[o[0] for o in outs],
        out_shape=[o[1] for o in outs],
        compiler_params=_cparams(("parallel", "parallel")),
        name="inproj_ctx" if is_ctx else "inproj",
    )(x, mod, n1w, w_in, qnw, knw, cos, sin, gmat, alog, dtb)


HALO = 8


def _conv_kernel(n_norm, x_ref, prev_ref, next_ref, w_ref, o_ref, win_ref):
    i = pl.program_id(1)
    nt = pl.num_programs(1)
    tc = x_ref.shape[0]
    win_ref[0:HALO, :] = jnp.where(i > 0, prev_ref[...], 0.0)
    win_ref[HALO:HALO + tc, :] = x_ref[...]
    win_ref[HALO + tc:, :] = jnp.where(i < nt - 1, next_ref[...], 0.0)
    for h in range(x_ref.shape[1] // LANES):
        cols = slice(h * LANES, (h + 1) * LANES)
        acc = None
        for j in range(CONV_K):
            start = HALO - CONV_K // 2 + j
            term = win_ref[start:start + tc, cols] * w_ref[j:j + 1, cols]
            acc = term if acc is None else acc + term
        y = _silu(acc)
        if h < n_norm:
            y = y * lax.rsqrt(jnp.sum(y * y, axis=-1, keepdims=True) + NORM_EPS)
        o_ref[:, cols] = y


def _short_conv_norm(x, w, n_norm, tc):
    b, t, c = x.shape
    nt = t // tc
    per = tc // HALO
    last = t // HALO - 1
    return pl.pallas_call(
        functools.partial(_conv_kernel, n_norm),
        grid=(b, nt),
        in_specs=[
            pl.BlockSpec((None, tc, c), lambda bi, i: (bi, i, 0)),
            pl.BlockSpec((None, HALO, c), lambda bi, i: (bi, jnp.maximum(i * per - 1, 0), 0)),
            pl.BlockSpec((None, HALO, c), lambda bi, i: (bi, jnp.minimum((i + 1) * per, last), 0)),
            pl.BlockSpec(w.shape, lambda bi, i: (0, 0)),
        ],
        out_specs=pl.BlockSpec((None, tc, c), lambda bi, i: (bi, i, 0)),
        out_shape=jax.ShapeDtypeStruct((b, t, c), F32),
        scratch_shapes=[pltpu.VMEM((tc + 2 * HALO, c), F32)],
        compiler_params=_cparams(("parallel", "parallel")),
        name="short_conv",
    )(x, x, x, w)


def _attn_kernel(q_ref, kc_ref, vc_ref, k_ref, v_ref, o_ref, m_ref, acc_ref):
    ki = pl.program_id(2)
    nk = pl.num_programs(2)

    def process(kb, vb):
        reps = kb.shape[0] // LANES
        for h in range(ATTN_HEADS):
            qh = q_ref[:, h * LANES:(h + 1) * LANES]
            s = lax.dot_general(qh, kb, (((1,), (1,)), ((), ())), preferred_element_type=F32)
            m_prev = m_ref[h]
            m_new = jnp.maximum(m_prev, jnp.max(s, axis=-1, keepdims=True))
            alpha = jnp.exp(m_prev - m_new)
            p = jnp.exp(s - jnp.concatenate([m_new] * reps, axis=1))
            pv = jnp.dot(p.astype(BF16), vb, preferred_element_type=F32)
            acc_ref[h] = jnp.concatenate([alpha] * (V_EXT_W // LANES), axis=1) * acc_ref[h] + pv
            m_ref[h] = m_new

    @pl.when(ki == 0)
    def _():
        m_ref[...] = jnp.full(m_ref.shape, -jnp.inf, F32)
        acc_ref[...] = jnp.zeros(acc_ref.shape, F32)
        process(kc_ref[...], vc_ref[...])

    process(k_ref[...], v_ref[...])

    @pl.when(ki == nk - 1)
    def _():
        for h in range(ATTN_HEADS):
            j = h // ATTN_GROUP
            acc = acc_ref[h]
            num = acc[:, j * ATTN_HEAD_DIM:(j + 1) * ATTN_HEAD_DIM]
            den = acc[:, ATTN_KV_W + j * ATTN_HEAD_DIM:ATTN_KV_W + (j + 1) * ATTN_HEAD_DIM]
            o_ref[:, h * ATTN_HEAD_DIM:(h + 1) * ATTN_HEAD_DIM] = (num / den).astype(o_ref.dtype)


def _attention(q, k, v, kc, vc, tq, tk):
    b, t, _ = q.shape
    n_ctx = kc.shape[1]
    return pl.pallas_call(
        _attn_kernel,
        grid=(b, t // tq, t // tk),
        in_specs=[
            pl.BlockSpec((None, tq, Q_EXT_W), lambda bi, qi, ki: (bi, qi, 0)),
            pl.BlockSpec((None, n_ctx, ATTN_KV_W), lambda bi, qi, ki: (bi, 0, 0)),
            pl.BlockSpec((None, n_ctx, V_EXT_W), lambda bi, qi, ki: (bi, 0, 0)),
            pl.BlockSpec((None, tk, ATTN_KV_W), lambda bi, qi, ki: (bi, ki, 0)),
            pl.BlockSpec((None, tk, V_EXT_W), lambda bi, qi, ki: (bi, ki, 0)),
        ],
        out_specs=pl.BlockSpec((None, tq, ATTN_W), lambda bi, qi, ki: (bi, qi, 0)),
        out_shape=jax.ShapeDtypeStruct((b, t, ATTN_W), BF16),
        scratch_shapes=[pltpu.VMEM((ATTN_HEADS, tq, LANES), F32),
                        pltpu.VMEM((ATTN_HEADS, tq, V_EXT_W), F32)],
        compiler_params=_cparams(("parallel", "parallel", "arbitrary")),
        name="attention",
    )(q, kc, vc, k, v)


def _dot_nt(a, b):
    return lax.dot_general(a, b, (((1,), (1,)), ((), ())), preferred_element_type=F32)


def _dot_tn(a, b):
    return lax.dot_general(a, b, (((0,), (0,)), ((), ())), preferred_element_type=F32)


def _unit_tri_inverses(ns, row, col):
    eye = (row == col).astype(F32)
    ds = None
    s = 1
    while s < CHUNK:
        sh = s.bit_length() - 1
        join = ((row >> (sh + 1)) == (col >> (sh + 1))) & ((row >> sh) != (col >> sh))
        es = [jnp.where(join, n, 0.0) for n in ns]
        if s == 1:
            ds = [eye - e for e in es]
        else:
            eds = [_dot_bf16(e, d) for e, d in zip(es, ds)]
            ds = [d - _dot_bf16(d, ed) for d, ed in zip(ds, eds)]
        s *= 2
    return ds


def _gdn_kernel(has_q, *refs):
    if has_q:
        (xf_ref, xb_ref, gf_ref, gb_ref, tf_ref, tb_ref, s0_ref, of_ref, ob_ref, sfin_ref, s_ref) = refs
    else:
        (xf_ref, xb_ref, gf_ref, gb_ref, tf_ref, tb_ref, s0_ref, sfin_ref, s_ref) = refs
    i = pl.program_id(0)
    nc = pl.num_programs(0)
    nb = xf_ref.shape[0]

    @pl.when(i == 0)
    def _():
        s_ref[...] = s0_ref[...]

    row = lax.broadcasted_iota(jnp.int32, (CHUNK, CHUNK), 0)
    col = lax.broadcasted_iota(jnp.int32, (CHUNK, CHUNK), 1)
    off_k = DN_W if has_q else 0
    off_v = off_k + DN_W
    scale = DN_HEAD_DIM ** -0.5
    units = [(bi, d, h) for bi in range(nb) for d in range(N_DIR) for h in range(DN_HEADS)]

    ops = []
    for bi, d, h in units:
        x_ref, g_ref, t_ref = (xf_ref, gf_ref, tf_ref) if d == 0 else (xb_ref, gb_ref, tb_ref)
        incl = (row >= col) if d == 0 else (row <= col)
        strict = (row > col) if d == 0 else (row < col)
        u = d * DN_HEADS + h

        def gate_col(kind):
            return g_ref[bi, :, kind * N_GATE + u:kind * N_GATE + u + 1]

        beta, gc, e_gc, e_dec = gate_col(0), gate_col(1), gate_col(2), gate_col(3)
        e_tot = g_ref[bi, 0:1, 4 * N_GATE + u:4 * N_GATE + u + 1]
        gc_t = t_ref[bi, u:u + 1, :]
        head = lambda off: x_ref[bi, :, off + h * DN_HEAD_DIM:off + (h + 1) * DN_HEAD_DIM]
        k, v = head(off_k), head(off_v)
        k16 = k.astype(BF16)
        kb = k * beta
        decay = jnp.exp(jnp.where(incl, gc - gc_t, -jnp.inf))
        if has_q:
            q = head(0) * scale
            kq = _dot_nt(jnp.concatenate([kb, q], axis=0).astype(BF16), k16)
            kk, qk = kq[:CHUNK], kq[CHUNK:]
            intra = (qk * decay).astype(BF16)
            qd = (q * e_gc).astype(BF16)
        else:
            kk = _dot_nt(kb.astype(BF16), k16)
            intra = qd = None
        ops.append(dict(
            n=jnp.where(strict, kk * decay, 0.0),
            rhs=jnp.concatenate([v * beta, kb * e_gc], axis=1).astype(BF16),
            k_dec=(k * e_dec).astype(BF16), e_tot=e_tot, intra=intra, qd=qd))

    t_invs = _unit_tri_inverses([o["n"] for o in ops], row, col)
    uws = [jnp.dot(t.astype(BF16), o["rhs"], preferred_element_type=F32) for t, o in zip(t_invs, ops)]

    reads = []
    for idx, (o, uw) in enumerate(zip(ops, uws)):
        s_old = s_ref[idx]
        w = uw[:, DN_HEAD_DIM:].astype(BF16)
        lhs = jnp.concatenate([w, o["qd"]], axis=0) if has_q else w
        reads.append((s_old, jnp.dot(lhs, s_old.astype(BF16), preferred_element_type=F32)))

    for idx, ((bi, d, h), o, uw, (s_old, rd)) in enumerate(zip(units, ops, uws, reads)):
        v_new = (uw[:, :DN_HEAD_DIM] - rd[:CHUNK]).astype(BF16)
        s_ref[idx] = s_old * o["e_tot"] + _dot_tn(o["k_dec"], v_new)
        if has_q:
            out = rd[CHUNK:] + jnp.dot(o["intra"], v_new, preferred_element_type=F32)
            o_ref = of_ref if d == 0 else ob_ref
            o_ref[bi, :, h * DN_HEAD_DIM:(h + 1) * DN_HEAD_DIM] = out

    @pl.when(i == nc - 1)
    def _():
        sfin_ref[...] = s_ref[...]


def _gdn_scan(x, gates, gates_t, s0, has_q):
    b, t, c = x.shape
    nc = t // CHUNK
    n_units = b * N_GATE
    fwd = lambda i: (0, i, 0)
    bwd = lambda i: (0, nc - 1 - i, 0)
    state_spec = pl.BlockSpec((n_units, DN_HEAD_DIM, DN_HEAD_DIM), lambda i: (0, 0, 0))
    state_shape = jax.ShapeDtypeStruct((n_units, DN_HEAD_DIM, DN_HEAD_DIM), F32)
    out_specs = [state_spec]
    out_shape = [state_shape]
    if has_q:
        o_shape = jax.ShapeDtypeStruct((b, t, DN_W), F32)
        out_specs = [pl.BlockSpec((b, CHUNK, DN_W), fwd), pl.BlockSpec((b, CHUNK, DN_W), bwd)] + out_specs
        out_shape = [o_shape, o_shape] + out_shape
    return pl.pallas_call(
        functools.partial(_gdn_kernel, has_q),
        grid=(nc,),
        in_specs=[pl.BlockSpec((b, CHUNK, c), fwd), pl.BlockSpec((b, CHUNK, c), bwd),
                  pl.BlockSpec((b, CHUNK, GATE_W), fwd), pl.BlockSpec((b, CHUNK, GATE_W), bwd),
                  pl.BlockSpec((b, None, N_GATE, CHUNK), lambda i: (0, i, 0, 0)),
                  pl.BlockSpec((b, None, N_GATE, CHUNK), lambda i: (0, nc - 1 - i, 0, 0)),
                  state_spec],
        out_specs=out_specs,
        out_shape=out_shape,
        scratch_shapes=[pltpu.VMEM((n_units, DN_HEAD_DIM, DN_HEAD_DIM), F32)],
        compiler_params=_cparams(("arbitrary",)),
        name="gdn_scan" if has_q else "gdn_scan_ctx",
    )(x, x, gates, gates, gates_t, gates_t, s0)


def _outproj_kernel(x_ref, oa_ref, of_ref, ob_ref, sz_ref, mod_ref, dnw_ref, w_ref, o_ref):
    o_dn = of_ref[...] + ob_ref[...]
    parts = [oa_ref[...]]
    for h in range(DN_HEADS):
        cols = slice(h * DN_HEAD_DIM, (h + 1) * DN_HEAD_DIM)
        seg = o_dn[:, cols]
        seg = seg * lax.rsqrt(jnp.mean(seg * seg, axis=-1, keepdims=True) + NORM_EPS) * dnw_ref[...]
        parts.append((seg * sz_ref[:, cols]).astype(BF16))
    mixed = jnp.dot(jnp.concatenate(parts, axis=1), w_ref[...], preferred_element_type=F32)
    o_ref[...] = x_ref[...] + mod_ref[2:3, :] * mixed


def _outproj(x, o_attn, o_f, o_b, sz, mod, dnw, w_out, tm):
    b, t, d = x.shape
    row = lambda bi, i: (bi, i, 0)
    full = lambda bi, i: (0, 0)
    return pl.pallas_call(
        _outproj_kernel,
        grid=(b, t // tm),
        in_specs=[pl.BlockSpec((None, tm, d), row),
                  pl.BlockSpec((None, tm, ATTN_W), row),
                  pl.BlockSpec((None, tm, DN_W), row),
                  pl.BlockSpec((None, tm, DN_W), row),
                  pl.BlockSpec((None, tm, DN_W), row),
                  pl.BlockSpec((None, N_MOD, d), lambda bi, i: (bi, 0, 0)),
                  pl.BlockSpec((1, DN_HEAD_DIM), full),
                  pl.BlockSpec(w_out.shape, full)],
        out_specs=pl.BlockSpec((None, tm, d), row),
        out_shape=jax.ShapeDtypeStruct((b, t, d), F32),
        compiler_params=_cparams(("parallel", "parallel")),
        name="outproj",
    )(x, o_attn, o_f, o_b, sz, mod, dnw, w_out)


def _mlp_kernel(ff_tile, x_ref, mod_ref, n2w_ref, w1_ref, w2_ref, o_ref):
    x = x_ref[...]
    y = x * lax.rsqrt(jnp.mean(x * x, axis=-1, keepdims=True) + NORM_EPS) * n2w_ref[...]
    h = (y * (1.0 + mod_ref[4:5, :]) + mod_ref[3:4, :]).astype(BF16)
    acc = None
    for c in range(w1_ref.shape[1] // ff_tile):
        a = jnp.dot(h, w1_ref[:, c * ff_tile:(c + 1) * ff_tile], preferred_element_type=F32)
        a = jnp.square(jnp.maximum(a, 0.0)).astype(BF16)
        part = jnp.dot(a, w2_ref[c * ff_tile:(c + 1) * ff_tile, :], preferred_element_type=F32)
        acc = part if acc is None else acc + part
    o_ref[...] = x + mod_ref[5:6, :] * acc


def _mlp(x, mod, n2w, w1, w2, tm, ff_tile):
    b, t, d = x.shape
    row = lambda bi, i: (bi, i, 0)
    full = lambda bi, i: (0, 0)
    return pl.pallas_call(
        functools.partial(_mlp_kernel, ff_tile),
        grid=(b, t // tm),
        in_specs=[pl.BlockSpec((None, tm, d), row),
                  pl.BlockSpec((None, N_MOD, d), lambda bi, i: (bi, 0, 0)),
                  pl.BlockSpec((1, d), full),
                  pl.BlockSpec(w1.shape, full, pipeline_mode=pl.Buffered(1)),
                  pl.BlockSpec(w2.shape, full, pipeline_mode=pl.Buffered(1))],
        out_specs=pl.BlockSpec((None, tm, d), row),
        out_shape=jax.ShapeDtypeStruct((b, t, d), F32),
        compiler_params=_cparams(("parallel", "parallel")),
        name="mlp",
    )(x, mod, n2w, w1, w2)


def _rope_tables(t):
    half = ATTN_HEAD_DIM // 4
    inv = ROPE_THETA ** (-jnp.arange(half, dtype=F32) / half)
    pos = jnp.arange(t, dtype=jnp.int32)
    ang_r = (pos // GRID_W).astype(F32)[:, None] * inv[None, :]
    ang_c = (pos % GRID_W).astype(F32)[:, None] * inv[None, :]
    cos = jnp.concatenate([jnp.cos(ang_r)] * 2 + [jnp.cos(ang_c)] * 2, axis=1)
    sin = jnp.concatenate([-jnp.sin(ang_r), jnp.sin(ang_r), -jnp.sin(ang_c), jnp.sin(ang_c)], axis=1)
    rep = LANES // ATTN_HEAD_DIM
    return jnp.tile(cos, (1, rep)), jnp.tile(sin, (1, rep))


def _pad_lanes(vec, offset):
    return jnp.zeros((1, GATE_W), F32).at[0, offset:offset + vec.size].set(vec.reshape(-1).astype(F32))


def kernel(x, c, ctx, c_ctx, w_mod, b_mod, norm1_w, w_in, q_norm_w, k_norm_w, conv_w, a_log, dt_bias,
           dn_norm_w, w_out, norm2_w, w_mlp1, w_mlp2):
    b, t, d = x.shape
    n_ctx = ctx.shape[1]
    depth = w_mod.shape[0]
    cos, sin = _rope_tables(t)
    head_id = jnp.arange(ATTN_W, dtype=jnp.int32) // ATTN_HEAD_DIM
    gmat = (head_id[:, None] == head_id[None, :]).astype(BF16)
    c_rows = jnp.zeros((8, d), F32).at[:b].set(c).at[b].set(c_ctx)
    tm = min(512, t)
    for l in range(depth):
        mod = _modulation(c_rows, w_mod[l], b_mod[l]).reshape(8, N_MOD, d)
        w_in_p = jnp.pad(w_in[l], ((0, 0), (0, IN_COLS_PAD - w_in.shape[2]))).astype(BF16)
        consts = (norm1_w[l].reshape(1, d), w_in_p,
                  jnp.tile(q_norm_w[l], ATTN_HEADS).reshape(1, ATTN_W),
                  jnp.tile(k_norm_w[l], ATTN_KV_HEADS).reshape(1, ATTN_KV_W),
                  cos, sin, gmat, _pad_lanes(a_log[l], N_GATE), _pad_lanes(dt_bias[l], N_GATE))
        q_a, k_a, v_a, dqkv, sz, gates, gates_t = _inproj(x, mod, lambda bi, i: (bi, 0, 0), False, tm, consts)
        k_c, v_c, cdkv, cgates, cgates_t = _inproj(ctx, mod, lambda bi, i: (b, 0, 0), True, n_ctx, consts)

        o_attn = _attention(q_a, k_a, v_a, k_c, v_c, min(512, t), min(1024, t))

        qkv = _short_conv_norm(dqkv, conv_w[l], 2 * DN_HEADS, tm)
        ckv = _short_conv_norm(cdkv, conv_w[l][:, DN_W:], DN_HEADS, n_ctx)
        s_zero = jnp.zeros((b * N_GATE, DN_HEAD_DIM, DN_HEAD_DIM), F32)
        (s_ctx,) = _gdn_scan(ckv, cgates, cgates_t, s_zero, False)
        o_f, o_b, _ = _gdn_scan(qkv, gates, gates_t, s_ctx, True)

        x = _outproj(x, o_attn, o_f, o_b, sz, mod, dn_norm_w[l].reshape(1, DN_HEAD_DIM),
                     w_out[l].astype(BF16), tm)
        x = _mlp(x, mod, norm2_w[l].reshape(1, d), w_mlp1[l].astype(BF16), w_mlp2[l].astype(BF16),
                 tm, 1024)
    return x
```

```python
import functools

import jax
import jax.numpy as jnp
from jax import lax
from jax.experimental import pallas as pl
from jax.experimental.pallas import tpu as pltpu

F32 = jnp.float32
BF16 = jnp.bfloat16

GRID_W = 64
ATTN_HEAD_DIM = 64
ATTN_HEADS = 8
ATTN_KV_HEADS = 2
ATTN_GROUP = ATTN_HEADS // ATTN_KV_HEADS
ATTN_W = ATTN_HEADS * ATTN_HEAD_DIM
ATTN_KV_W = ATTN_KV_HEADS * ATTN_HEAD_DIM
DN_HEAD_DIM = 128
DN_HEADS = 4
DN_W = DN_HEADS * DN_HEAD_DIM
N_DIR = 2
CONV_K = 5
CHUNK = 64
N_MOD = 6
ROPE_THETA = 10000.0
NORM_EPS = 1e-6
LANES = 128
GATE_W = LANES
N_GATE = N_DIR * DN_HEADS

OFF_Q = 0
OFF_K = OFF_Q + ATTN_W
OFF_V = OFF_K + ATTN_KV_W
OFF_DQKV = OFF_V + ATTN_KV_W
OFF_Z = OFF_DQKV + 3 * DN_W
OFF_GATE = OFF_Z + DN_W
IN_COLS_PAD = OFF_GATE + GATE_W

Q_EXT_W = ATTN_HEADS * LANES
V_EXT_W = 2 * LANES

VMEM_LIMIT = 56 * 1024 * 1024


def _cparams(sem):
    return pltpu.CompilerParams(dimension_semantics=sem, vmem_limit_bytes=VMEM_LIMIT)


def _dot_bf16(a, b):
    return jnp.dot(a.astype(BF16), b.astype(BF16), preferred_element_type=F32)


def _dot_f32(a, b):
    return jnp.dot(a, b, preferred_element_type=F32, precision=lax.Precision.HIGHEST)


def _split_bf16(a):
    hi = a.astype(BF16)
    lo = (a - hi.astype(F32)).astype(BF16)
    return hi, lo


def _silu(x):
    return x * jax.nn.sigmoid(x)


def _mod_kernel(c_ref, w_ref, b_ref, o_ref):
    o_ref[...] = _dot_f32(_silu(c_ref[...]), w_ref[...]) + b_ref[...]


def _modulation(c_rows, w_mod, b_mod):
    r, d = c_rows.shape
    n = w_mod.shape[1]
    tn = 1024
    return pl.pallas_call(
        _mod_kernel,
        grid=(n // tn,),
        in_specs=[pl.BlockSpec((r, d), lambda j: (0, 0)),
                  pl.BlockSpec((d, tn), lambda j: (0, j)),
                  pl.BlockSpec((1, tn), lambda j: (0, j))],
        out_specs=pl.BlockSpec((r, tn), lambda j: (0, j)),
        out_shape=jax.ShapeDtypeStruct((r, n), F32),
        compiler_params=_cparams(("arbitrary",)),
        name="modulation",
    )(c_rows, w_mod, b_mod.reshape(1, n))


def _group_mean_sq(a, gmat, group):
    hi, lo = _split_bf16(a * a)
    d = functools.partial(jnp.dot, preferred_element_type=F32)
    return (d(hi, gmat) + d(lo, gmat)) * (1.0 / group)


def _rope(a, cos, sin):
    parts = []
    for c in range(a.shape[1] // LANES):
        ac = a[:, c * LANES:(c + 1) * LANES]
        lane = lax.broadcasted_iota(jnp.int32, ac.shape, 1)
        nxt = pltpu.roll(ac, LANES - 16, 1)
        prv = pltpu.roll(ac, 16, 1)
        parts.append(jnp.where((lane & 31) < 16, nxt, prv))
    partner = parts[0] if len(parts) == 1 else jnp.concatenate(parts, axis=1)
    return a * cos + partner * sin


def _chunk_scans(x):
    n = x.shape[0]
    pos = lax.broadcasted_iota(jnp.int32, x.shape, 0) & (CHUNK - 1)
    fwd, rev = x, x
    s = 1
    while s < CHUNK:
        fwd = fwd + jnp.where(pos >= s, pltpu.roll(fwd, s, 0), 0.0)
        rev = rev + jnp.where(pos + s < CHUNK, pltpu.roll(rev, n - s, 0), 0.0)
        s *= 2
    return fwd, rev


def _gate_maps(raw, alog, dtb):
    lane = lax.broadcasted_iota(jnp.int32, raw.shape, 1)
    beta = jax.nn.sigmoid(raw)
    y = raw + dtb
    softplus = jnp.maximum(y, 0.0) + jnp.log1p(jnp.exp(-jnp.abs(y)))
    g = -jnp.exp(alog) * softplus
    fwd, rev = _chunk_scans(g)
    gc = jnp.where(lane < N_GATE + DN_HEADS, fwd, rev)
    total = fwd + rev - g
    e_gc = pltpu.roll(jnp.exp(gc), N_GATE, 1)
    e_dec = pltpu.roll(jnp.exp(total - gc), 2 * N_GATE, 1)
    e_tot = pltpu.roll(jnp.exp(total), 3 * N_GATE, 1)
    out = jnp.where(lane < N_GATE, beta,
                    jnp.where(lane < 2 * N_GATE, gc,
                              jnp.where(lane < 3 * N_GATE, e_gc,
                                        jnp.where(lane < 4 * N_GATE, e_dec, e_tot))))
    return out, gc


def _inproj_kernel(is_ctx, x_ref, mod_ref, n1w_ref, w_ref, qnw_ref, knw_ref, cos_ref, sin_ref,
                   gmat_ref, alog_ref, dtb_ref, *outs):
    x = x_ref[...]
    y = x * lax.rsqrt(jnp.mean(x * x, axis=-1, keepdims=True) + NORM_EPS) * n1w_ref[...]
    h = (y * (1.0 + mod_ref[1:2, :]) + mod_ref[0:1, :]).astype(BF16)

    def proj(lo, hi):
        return jnp.dot(h, w_ref[:, lo:hi], preferred_element_type=F32)

    kk = proj(OFF_K, OFF_V)
    kk = kk * lax.rsqrt(_group_mean_sq(kk, gmat_ref[:ATTN_KV_W, :ATTN_KV_W], ATTN_HEAD_DIM)
                        + NORM_EPS) * knw_ref[...]
    vv = proj(OFF_V, OFF_DQKV)
    gates, gc = _gate_maps(proj(OFF_GATE, IN_COLS_PAD), alog_ref[...], dtb_ref[...])
    if is_ctx:
        k_out, v_out, dkv_out, gate_out, gt_out = outs
        dkv_out[...] = proj(OFF_DQKV + DN_W, OFF_Z)
    else:
        q_out, k_out, v_out, dqkv_out, sz_out, gate_out, gt_out = outs
        cos = cos_ref[...]
        sin = sin_ref[...]
        kk = _rope(kk, cos, sin)
        qq = proj(OFF_Q, OFF_K)
        qq = qq * lax.rsqrt(_group_mean_sq(qq, gmat_ref[...], ATTN_HEAD_DIM) + NORM_EPS) * qnw_ref[...]
        rep = ATTN_W // LANES
        qq = _rope(qq, jnp.concatenate([cos] * rep, axis=1), jnp.concatenate([sin] * rep, axis=1))
        qb = (qq * (ATTN_HEAD_DIM ** -0.5)).astype(q_out.dtype)
        zeros = jnp.zeros((qb.shape[0], ATTN_HEAD_DIM), q_out.dtype)
        for hd in range(ATTN_HEADS):
            seg = qb[:, hd * ATTN_HEAD_DIM:(hd + 1) * ATTN_HEAD_DIM]
            pair = [seg, zeros] if hd // ATTN_GROUP == 0 else [zeros, seg]
            q_out[:, hd * LANES:(hd + 1) * LANES] = jnp.concatenate(pair, axis=1)
        dqkv_out[...] = proj(OFF_DQKV, OFF_Z)
        sz_out[...] = _silu(proj(OFF_Z, OFF_GATE))
    k_out[...] = kk.astype(k_out.dtype)
    v_out[:, :ATTN_KV_W] = vv.astype(v_out.dtype)
    v_out[:, ATTN_KV_W:] = jnp.ones((vv.shape[0], V_EXT_W - ATTN_KV_W), v_out.dtype)
    gate_out[...] = gates
    gc_t = gc.T
    for ch in range(gt_out.shape[0]):
        gt_out[ch] = gc_t[N_GATE:2 * N_GATE, ch * CHUNK:(ch + 1) * CHUNK]


def _inproj(x, mod, mod_index, is_ctx, tm, consts):
    b, t, d = x.shape
    n1w, w_in, qnw, knw, cos, sin, gmat, alog, dtb = consts
    nt = t // tm
    row = lambda bi, i: (bi, i, 0)
    full = lambda bi, i: (0, 0)
    in_specs = [
        pl.BlockSpec((None, tm, d), row),
        pl.BlockSpec((None, N_MOD, d), mod_index),
        pl.BlockSpec((1, d), full),
        pl.BlockSpec(w_in.shape, full),
        pl.BlockSpec(qnw.shape, full),
        pl.BlockSpec(knw.shape, full),
        pl.BlockSpec((tm, LANES), lambda bi, i: (i, 0)),
        pl.BlockSpec((tm, LANES), lambda bi, i: (i, 0)),
        pl.BlockSpec(gmat.shape, full),
        pl.BlockSpec((1, GATE_W), full),
        pl.BlockSpec((1, GATE_W), full),
    ]

    def out(width, dtype):
        return pl.BlockSpec((None, tm, width), row), jax.ShapeDtypeStruct((b, t, width), dtype)

    if is_ctx:
        outs = [out(ATTN_KV_W, BF16), out(V_EXT_W, BF16), out(2 * DN_W, F32), out(GATE_W, F32)]
    else:
        outs = [out(Q_EXT_W, BF16), out(ATTN_KV_W, BF16), out(V_EXT_W, BF16),
                out(3 * DN_W, F32), out(DN_W, F32), out(GATE_W, F32)]
    outs.append((pl.BlockSpec((None, tm // CHUNK, N_GATE, CHUNK), lambda bi, i: (bi, i, 0, 0)),
                 jax.ShapeDtypeStruct((b, t // CHUNK, N_GATE, CHUNK), F32)))
    return pl.pallas_call(
        functools.partial(_inproj_kernel, is_ctx),
        grid=(b, nt),
        in_specs=in_specs,
        out_specs=[o[0] for o in outs],
        out_shape=[o[1] for o in outs],
        compiler_params=_cparams(("parallel", "parallel")),
        name="inproj_ctx" if is_ctx else "inproj",
    )(x, mod, n1w, w_in, qnw, knw, cos, sin, gmat, alog, dtb)


HALO = 8


def _conv_kernel(n_norm, x_ref, prev_ref, next_ref, w_ref, o_ref, win_ref):
    i = pl.program_id(1)
    nt = pl.num_programs(1)
    tc = x_ref.shape[0]
    win_ref[0:HALO, :] = jnp.where(i > 0, prev_ref[...], 0.0)
    win_ref[HALO:HALO + tc, :] = x_ref[...]
    win_ref[HALO + tc:, :] = jnp.where(i < nt - 1, next_ref[...], 0.0)
    for h in range(x_ref.shape[1] // LANES):
        cols = slice(h * LANES, (h + 1) * LANES)
        acc = None
        for j in range(CONV_K):
            start = HALO - CONV_K // 2 + j
            term = win_ref[start:start + tc, cols] * w_ref[j:j + 1, cols]
            acc = term if acc is None else acc + term
        y = _silu(acc)
        if h < n_norm:
            y = y * lax.rsqrt(jnp.sum(y * y, axis=-1, keepdims=True) + NORM_EPS)
        o_ref[:, cols] = y


def _short_conv_norm(x, w, n_norm, tc):
    b, t, c = x.shape
    nt = t // tc
    per = tc // HALO
    last = t // HALO - 1
    return pl.pallas_call(
        functools.partial(_conv_kernel, n_norm),
        grid=(b, nt),
        in_specs=[
            pl.BlockSpec((None, tc, c), lambda bi, i: (bi, i, 0)),
            pl.BlockSpec((None, HALO, c), lambda bi, i: (bi, jnp.maximum(i * per - 1, 0), 0)),
            pl.BlockSpec((None, HALO, c), lambda bi, i: (bi, jnp.minimum((i + 1) * per, last), 0)),
            pl.BlockSpec(w.shape, lambda bi, i: (0, 0)),
        ],
        out_specs=pl.BlockSpec((None, tc, c), lambda bi, i: (bi, i, 0)),
        out_shape=jax.ShapeDtypeStruct((b, t, c), F32),
        scratch_shapes=[pltpu.VMEM((tc + 2 * HALO, c), F32)],
        compiler_params=_cparams(("parallel", "parallel")),
        name="short_conv",
    )(x, x, x, w)


def _attn_kernel(online, q_ref, kc_ref, vc_ref, k_ref, v_ref, o_ref, bad_ref, m_ref, acc_ref):
    ki = pl.program_id(2)
    nk = pl.num_programs(2)

    def block(kb, vb, first):
        reps = kb.shape[0] // LANES

        def scores(h):
            return lax.dot_general(q_ref[:, h * LANES:(h + 1) * LANES], kb, (((1,), (1,)), ((), ())),
                                   preferred_element_type=F32)

        s_next = scores(0)
        for h in range(ATTN_HEADS):
            s = s_next
            if h + 1 < ATTN_HEADS:
                s_next = scores(h + 1)
            if first:
                m = jnp.broadcast_to(jnp.max(s, axis=-1, keepdims=True), m_ref.shape[1:])
                m_ref[h] = m
            elif online:
                m_prev = m_ref[h]
                m = jnp.maximum(m_prev, jnp.max(s, axis=-1, keepdims=True))
                alpha = jnp.exp(m_prev - m)
                m_ref[h] = m
            else:
                m = m_ref[h]
            p = jnp.exp(s - jnp.concatenate([m] * reps, axis=1))
            pv = jnp.dot(p.astype(BF16), vb, preferred_element_type=F32)
            if first:
                acc_ref[h] = pv
            elif online:
                acc_ref[h] = jnp.concatenate([alpha] * (V_EXT_W // LANES), axis=1) * acc_ref[h] + pv
            else:
                acc_ref[h] += pv

    @pl.when(ki == 0)
    def _():
        block(kc_ref[...], vc_ref[...], True)

    block(k_ref[...], v_ref[...], False)

    @pl.when(ki == nk - 1)
    def _():
        bad = jnp.zeros((1, 1), F32)
        for h in range(ATTN_HEADS):
            j = h // ATTN_GROUP
            acc = acc_ref[h]
            num = acc[:, j * ATTN_HEAD_DIM:(j + 1) * ATTN_HEAD_DIM]
            den = acc[:, ATTN_KV_W + j * ATTN_HEAD_DIM:ATTN_KV_W + (j + 1) * ATTN_HEAD_DIM]
            out = num / den
            o_ref[:, h * ATTN_HEAD_DIM:(h + 1) * ATTN_HEAD_DIM] = out.astype(o_ref.dtype)
            bad = jnp.maximum(bad, jnp.max(jnp.where(jnp.isfinite(out), 0.0, 1.0), keepdims=True))
        bad_ref[...] = jnp.broadcast_to(bad, bad_ref.shape)


def _attention(online, q, k, v, kc, vc, tq, tk):
    b, t, _ = q.shape
    n_ctx = kc.shape[1]
    nq = t // tq
    return pl.pallas_call(
        functools.partial(_attn_kernel, online),
        grid=(b, nq, t // tk),
        in_specs=[
            pl.BlockSpec((None, tq, Q_EXT_W), lambda bi, qi, ki: (bi, qi, 0)),
            pl.BlockSpec((None, n_ctx, ATTN_KV_W), lambda bi, qi, ki: (bi, 0, 0)),
            pl.BlockSpec((None, n_ctx, V_EXT_W), lambda bi, qi, ki: (bi, 0, 0)),
            pl.BlockSpec((None, tk, ATTN_KV_W), lambda bi, qi, ki: (bi, ki, 0)),
            pl.BlockSpec((None, tk, V_EXT_W), lambda bi, qi, ki: (bi, ki, 0)),
        ],
        out_specs=[pl.BlockSpec((None, tq, ATTN_W), lambda bi, qi, ki: (bi, qi, 0)),
                   pl.BlockSpec((None, None, 8, LANES), lambda bi, qi, ki: (bi, qi, 0, 0))],
        out_shape=[jax.ShapeDtypeStruct((b, t, ATTN_W), BF16),
                   jax.ShapeDtypeStruct((b, nq, 8, LANES), F32)],
        scratch_shapes=[pltpu.VMEM((ATTN_HEADS, tq, LANES), F32),
                        pltpu.VMEM((ATTN_HEADS, tq, V_EXT_W), F32)],
        compiler_params=_cparams(("parallel", "parallel", "arbitrary")),
        name="attention_online" if online else "attention",
    )(q, kc, vc, k, v)


def _dot_nt(a, b):
    return lax.dot_general(a, b, (((1,), (1,)), ((), ())), preferred_element_type=F32)


def _dot_tn(a, b):
    return lax.dot_general(a, b, (((0,), (0,)), ((), ())), preferred_element_type=F32)


def _unit_tri_inverses(ns, row, col):
    eye = (row == col).astype(F32)
    ds = None
    s = 1
    while s < CHUNK:
        sh = s.bit_length() - 1
        join = ((row >> (sh + 1)) == (col >> (sh + 1))) & ((row >> sh) != (col >> sh))
        es = [jnp.where(join, n, 0.0) for n in ns]
        if s == 1:
            ds = [eye - e for e in es]
        else:
            eds = [_dot_bf16(e, d) for e, d in zip(es, ds)]
            ds = [d - _dot_bf16(d, ed) for d, ed in zip(ds, eds)]
        s *= 2
    return ds


def _gdn_kernel(has_q, *refs):
    if has_q:
        (xf_ref, xb_ref, gf_ref, gb_ref, tf_ref, tb_ref, s0_ref, of_ref, ob_ref, sfin_ref, s_ref) = refs
    else:
        (xf_ref, xb_ref, gf_ref, gb_ref, tf_ref, tb_ref, s0_ref, sfin_ref, s_ref) = refs
    i = pl.program_id(0)
    nc = pl.num_programs(0)
    nb = xf_ref.shape[0]

    @pl.when(i == 0)
    def _():
        s_ref[...] = s0_ref[...]

    row = lax.broadcasted_iota(jnp.int32, (CHUNK, CHUNK), 0)
    col = lax.broadcasted_iota(jnp.int32, (CHUNK, CHUNK), 1)
    off_k = DN_W if has_q else 0
    off_v = off_k + DN_W
    scale = DN_HEAD_DIM ** -0.5
    units = [(bi, d, h) for bi in range(nb) for d in range(N_DIR) for h in range(DN_HEADS)]

    ops = []
    for bi, d, h in units:
        x_ref, g_ref, t_ref = (xf_ref, gf_ref, tf_ref) if d == 0 else (xb_ref, gb_ref, tb_ref)
        incl = (row >= col) if d == 0 else (row <= col)
        strict = (row > col) if d == 0 else (row < col)
        u = d * DN_HEADS + h

        def gate_col(kind):
            return g_ref[bi, :, kind * N_GATE + u:kind * N_GATE + u + 1]

        beta, gc, e_gc, e_dec = gate_col(0), gate_col(1), gate_col(2), gate_col(3)
        e_tot = g_ref[bi, 0:1, 4 * N_GATE + u:4 * N_GATE + u + 1]
        gc_t = t_ref[bi, u:u + 1, :]
        head = lambda off: x_ref[bi, :, off + h * DN_HEAD_DIM:off + (h + 1) * DN_HEAD_DIM]
        k, v = head(off_k), head(off_v)
        k16 = k.astype(BF16)
        kb = k * beta
        decay = jnp.exp(jnp.where(incl, gc - gc_t, -jnp.inf))
        if has_q:
            q = head(0) * scale
            kq = _dot_nt(jnp.concatenate([kb, q], axis=0).astype(BF16), k16)
            kk, qk = kq[:CHUNK], kq[CHUNK:]
            intra = (qk * decay).astype(BF16)
            qd = (q * e_gc).astype(BF16)
        else:
            kk = _dot_nt(kb.astype(BF16), k16)
            intra = qd = None
        ops.append(dict(
            n=jnp.where(strict, kk * decay, 0.0),
            rhs=jnp.concatenate([v * beta, kb * e_gc], axis=1).astype(BF16),
            k_dec=(k * e_dec).astype(BF16), e_tot=e_tot, intra=intra, qd=qd))

    t_invs = _unit_tri_inverses([o["n"] for o in ops], row, col)
    uws = [jnp.dot(t.astype(BF16), o["rhs"], preferred_element_type=F32) for t, o in zip(t_invs, ops)]

    reads = []
    for idx, (o, uw) in enumerate(zip(ops, uws)):
        s_old = s_ref[idx]
        w = uw[:, DN_HEAD_DIM:].astype(BF16)
        lhs = jnp.concatenate([w, o["qd"]], axis=0) if has_q else w
        reads.append((s_old, jnp.dot(lhs, s_old.astype(BF16), preferred_element_type=F32)))

    for idx, ((bi, d, h), o, uw, (s_old, rd)) in enumerate(zip(units, ops, uws, reads)):
        v_new = (uw[:, :DN_HEAD_DIM] - rd[:CHUNK]).astype(BF16)
        s_ref[idx] = s_old * o["e_tot"] + _dot_tn(o["k_dec"], v_new)
        if has_q:
            out = rd[CHUNK:] + jnp.dot(o["intra"], v_new, preferred_element_type=F32)
            o_ref = of_ref if d == 0 else ob_ref
            o_ref[bi, :, h * DN_HEAD_DIM:(h + 1) * DN_HEAD_DIM] = out

    @pl.when(i == nc - 1)
    def _():
        sfin_ref[...] = s_ref[...]


def _gdn_scan(x, gates, gates_t, s0, has_q):
    b, t, c = x.shape
    nc = t // CHUNK
    n_units = b * N_GATE
    fwd = lambda i: (0, i, 0)
    bwd = lambda i: (0, nc - 1 - i, 0)
    state_spec = pl.BlockSpec((n_units, DN_HEAD_DIM, DN_HEAD_DIM), lambda i: (0, 0, 0))
    state_shape = jax.ShapeDtypeStruct((n_units, DN_HEAD_DIM, DN_HEAD_DIM), F32)
    out_specs = [state_spec]
    out_shape = [state_shape]
    if has_q:
        o_shape = jax.ShapeDtypeStruct((b, t, DN_W), F32)
        out_specs = [pl.BlockSpec((b, CHUNK, DN_W), fwd), pl.BlockSpec((b, CHUNK, DN_W), bwd)] + out_specs
        out_shape = [o_shape, o_shape] + out_shape
    return pl.pallas_call(
        functools.partial(_gdn_kernel, has_q),
        grid=(nc,),
        in_specs=[pl.BlockSpec((b, CHUNK, c), fwd), pl.BlockSpec((b, CHUNK, c), bwd),
                  pl.BlockSpec((b, CHUNK, GATE_W), fwd), pl.BlockSpec((b, CHUNK, GATE_W), bwd),
                  pl.BlockSpec((b, None, N_GATE, CHUNK), lambda i: (0, i, 0, 0)),
                  pl.BlockSpec((b, None, N_GATE, CHUNK), lambda i: (0, nc - 1 - i, 0, 0)),
                  state_spec],
        out_specs=out_specs,
        out_shape=out_shape,
        scratch_shapes=[pltpu.VMEM((n_units, DN_HEAD_DIM, DN_HEAD_DIM), F32)],
        compiler_params=_cparams(("arbitrary",)),
        name="gdn_scan" if has_q else "gdn_scan_ctx",
    )(x, x, gates, gates, gates_t, gates_t, s0)


def _outproj_kernel(x_ref, oa_ref, of_ref, ob_ref, sz_ref, mod_ref, dnw_ref, w_ref, o_ref):
    o_dn = of_ref[...] + ob_ref[...]
    parts = [oa_ref[...]]
    for h in range(DN_HEADS):
        cols = slice(h * DN_HEAD_DIM, (h + 1) * DN_HEAD_DIM)
        seg = o_dn[:, cols]
        seg = seg * lax.rsqrt(jnp.mean(seg * seg, axis=-1, keepdims=True) + NORM_EPS) * dnw_ref[...]
        parts.append((seg * sz_ref[:, cols]).astype(BF16))
    mixed = jnp.dot(jnp.concatenate(parts, axis=1), w_ref[...], preferred_element_type=F32)
    o_ref[...] = x_ref[...] + mod_ref[2:3, :] * mixed


def _outproj(x, o_attn, o_f, o_b, sz, mod, dnw, w_out, tm):
    b, t, d = x.shape
    row = lambda bi, i: (bi, i, 0)
    full = lambda bi, i: (0, 0)
    return pl.pallas_call(
        _outproj_kernel,
        grid=(b, t // tm),
        in_specs=[pl.BlockSpec((None, tm, d), row),
                  pl.BlockSpec((None, tm, ATTN_W), row),
                  pl.BlockSpec((None, tm, DN_W), row),
                  pl.BlockSpec((None, tm, DN_W), row),
                  pl.BlockSpec((None, tm, DN_W), row),
                  pl.BlockSpec((None, N_MOD, d), lambda bi, i: (bi, 0, 0)),
                  pl.BlockSpec((1, DN_HEAD_DIM), full),
                  pl.BlockSpec(w_out.shape, full)],
        out_specs=pl.BlockSpec((None, tm, d), row),
        out_shape=jax.ShapeDtypeStruct((b, t, d), F32),
        compiler_params=_cparams(("parallel", "parallel")),
        name="outproj",
    )(x, o_attn, o_f, o_b, sz, mod, dnw, w_out)


def _mlp_kernel(ff_tile, x_ref, mod_ref, n2w_ref, w1_ref, w2_ref, o_ref):
    x = x_ref[...]
    y = x * lax.rsqrt(jnp.mean(x * x, axis=-1, keepdims=True) + NORM_EPS) * n2w_ref[...]
    h = (y * (1.0 + mod_ref[4:5, :]) + mod_ref[3:4, :]).astype(BF16)
    acc = None
    for c in range(w1_ref.shape[1] // ff_tile):
        a = jnp.dot(h, w1_ref[:, c * ff_tile:(c + 1) * ff_tile], preferred_element_type=F32)
        a = jnp.square(jnp.maximum(a, 0.0)).astype(BF16)
        part = jnp.dot(a, w2_ref[c * ff_tile:(c + 1) * ff_tile, :], preferred_element_type=F32)
        acc = part if acc is None else acc + part
    o_ref[...] = x + mod_ref[5:6, :] * acc


def _mlp(x, mod, n2w, w1, w2, tm, ff_tile):
    b, t, d = x.shape
    row = lambda bi, i: (bi, i, 0)
    full = lambda bi, i: (0, 0)
    return pl.pallas_call(
        functools.partial(_mlp_kernel, ff_tile),
        grid=(b, t // tm),
        in_specs=[pl.BlockSpec((None, tm, d), row),
                  pl.BlockSpec((None, N_MOD, d), lambda bi, i: (bi, 0, 0)),
                  pl.BlockSpec((1, d), full),
                  pl.BlockSpec(w1.shape, full, pipeline_mode=pl.Buffered(1)),
                  pl.BlockSpec(w2.shape, full, pipeline_mode=pl.Buffered(1))],
        out_specs=pl.BlockSpec((None, tm, d), row),
        out_shape=jax.ShapeDtypeStruct((b, t, d), F32),
        compiler_params=_cparams(("parallel", "parallel")),
        name="mlp",
    )(x, mod, n2w, w1, w2)


def _rope_tables(t):
    half = ATTN_HEAD_DIM // 4
    inv = ROPE_THETA ** (-jnp.arange(half, dtype=F32) / half)
    pos = jnp.arange(t, dtype=jnp.int32)
    ang_r = (pos // GRID_W).astype(F32)[:, None] * inv[None, :]
    ang_c = (pos % GRID_W).astype(F32)[:, None] * inv[None, :]
    cos = jnp.concatenate([jnp.cos(ang_r)] * 2 + [jnp.cos(ang_c)] * 2, axis=1)
    sin = jnp.concatenate([-jnp.sin(ang_r), jnp.sin(ang_r), -jnp.sin(ang_c), jnp.sin(ang_c)], axis=1)
    rep = LANES // ATTN_HEAD_DIM
    return jnp.tile(cos, (1, rep)), jnp.tile(sin, (1, rep))


def _pad_lanes(vec, offset):
    return jnp.zeros((1, GATE_W), F32).at[0, offset:offset + vec.size].set(vec.reshape(-1).astype(F32))


def kernel(x, c, ctx, c_ctx, w_mod, b_mod, norm1_w, w_in, q_norm_w, k_norm_w, conv_w, a_log, dt_bias,
           dn_norm_w, w_out, norm2_w, w_mlp1, w_mlp2):
    b, t, d = x.shape
    n_ctx = ctx.shape[1]
    depth = w_mod.shape[0]
    cos, sin = _rope_tables(t)
    head_id = jnp.arange(ATTN_W, dtype=jnp.int32) // ATTN_HEAD_DIM
    gmat = (head_id[:, None] == head_id[None, :]).astype(BF16)
    c_rows = jnp.zeros((8, d), F32).at[:b].set(c).at[b].set(c_ctx)
    tm = min(512, t)
    for l in range(depth):
        mod = _modulation(c_rows, w_mod[l], b_mod[l]).reshape(8, N_MOD, d)
        w_in_p = jnp.pad(w_in[l], ((0, 0), (0, IN_COLS_PAD - w_in.shape[2]))).astype(BF16)
        consts = (norm1_w[l].reshape(1, d), w_in_p,
                  jnp.tile(q_norm_w[l], ATTN_HEADS).reshape(1, ATTN_W),
                  jnp.tile(k_norm_w[l], ATTN_KV_HEADS).reshape(1, ATTN_KV_W),
                  cos, sin, gmat, _pad_lanes(a_log[l], N_GATE), _pad_lanes(dt_bias[l], N_GATE))
        q_a, k_a, v_a, dqkv, sz, gates, gates_t = _inproj(x, mod, lambda bi, i: (bi, 0, 0), False, tm, consts)
        k_c, v_c, cdkv, cgates, cgates_t = _inproj(ctx, mod, lambda bi, i: (b, 0, 0), True, n_ctx, consts)

        attn_args = (q_a, k_a, v_a, k_c, v_c, min(512, t), min(1024, t))
        o_fixed, bad = _attention(False, *attn_args)
        o_attn = lax.cond(jnp.max(bad) > 0.0, lambda: _attention(True, *attn_args)[0], lambda: o_fixed)

        qkv = _short_conv_norm(dqkv, conv_w[l], 2 * DN_HEADS, tm)
        ckv = _short_conv_norm(cdkv, conv_w[l][:, DN_W:], DN_HEADS, n_ctx)
        s_zero = jnp.zeros((b * N_GATE, DN_HEAD_DIM, DN_HEAD_DIM), F32)
        (s_ctx,) = _gdn_scan(ckv, cgates, cgates_t, s_zero, False)
        o_f, o_b, _ = _gdn_scan(qkv, gates, gates_t, s_ctx, True)

        x = _outproj(x, o_attn, o_f, o_b, sz, mod, dn_norm_w[l].reshape(1, DN_HEAD_DIM),
                     w_out[l].astype(BF16), tm)
        x = _mlp(x, mod, norm2_w[l].reshape(1, d), w_mlp1[l].astype(BF16), w_mlp2[l].astype(BF16),
                 tm, 1024)
    return x
```

```python
import functools

import jax
import jax.numpy as jnp
from jax import lax
from jax.experimental import pallas as pl
from jax.experimental.pallas import tpu as pltpu

F32 = jnp.float32
BF16 = jnp.bfloat16

GRID_W = 64
ATTN_HEAD_DIM = 64
ATTN_HEADS = 8
ATTN_KV_HEADS = 2
ATTN_GROUP = ATTN_HEADS // ATTN_KV_HEADS
ATTN_W = ATTN_HEADS * ATTN_HEAD_DIM
ATTN_KV_W = ATTN_KV_HEADS * ATTN_HEAD_DIM
DN_HEAD_DIM = 128
DN_HEADS = 4
DN_W = DN_HEADS * DN_HEAD_DIM
N_DIR = 2
CONV_K = 5
CHUNK = 64
N_MOD = 6
ROPE_THETA = 10000.0
NORM_EPS = 1e-6
LANES = 128
GATE_W = LANES
N_GATE = N_DIR * DN_HEADS

OFF_Q = 0
OFF_K = OFF_Q + ATTN_W
OFF_V = OFF_K + ATTN_KV_W
OFF_DQKV = OFF_V + ATTN_KV_W
OFF_Z = OFF_DQKV + 3 * DN_W
OFF_GATE = OFF_Z + DN_W
IN_COLS_PAD = OFF_GATE + GATE_W

HALO = 8
Q_EXT_W = ATTN_HEADS * LANES
V_EXT_W = 2 * LANES

VMEM_LIMIT = 56 * 1024 * 1024


def _cparams(sem):
    return pltpu.CompilerParams(dimension_semantics=sem, vmem_limit_bytes=VMEM_LIMIT)


def _dot_bf16(a, b):
    return jnp.dot(a.astype(BF16), b.astype(BF16), preferred_element_type=F32)


def _dot_f32(a, b):
    return jnp.dot(a, b, preferred_element_type=F32, precision=lax.Precision.HIGHEST)


def _split_bf16(a):
    hi = a.astype(BF16)
    lo = (a - hi.astype(F32)).astype(BF16)
    return hi, lo


def _silu(x):
    return x * jax.nn.sigmoid(x)


def _mod_kernel(c_ref, w_ref, b_ref, o_ref):
    o_ref[...] = _dot_f32(_silu(c_ref[...]), w_ref[...]) + b_ref[...]


def _modulation(c_rows, w_mod, b_mod):
    r, d = c_rows.shape
    n = w_mod.shape[1]
    tn = 1024
    return pl.pallas_call(
        _mod_kernel,
        grid=(n // tn,),
        in_specs=[pl.BlockSpec((r, d), lambda j: (0, 0)),
                  pl.BlockSpec((d, tn), lambda j: (0, j)),
                  pl.BlockSpec((1, tn), lambda j: (0, j))],
        out_specs=pl.BlockSpec((r, tn), lambda j: (0, j)),
        out_shape=jax.ShapeDtypeStruct((r, n), F32),
        compiler_params=_cparams(("arbitrary",)),
        name="modulation",
    )(c_rows, w_mod, b_mod.reshape(1, n))


def _group_mean_sq(a, gmat, group):
    hi, lo = _split_bf16(a * a)
    return jnp.dot(jnp.concatenate([hi, lo], axis=1), jnp.concatenate([gmat, gmat], axis=0),
                   preferred_element_type=F32) * (1.0 / group)


def _rope(a, cos, sin):
    parts = []
    for c in range(a.shape[1] // LANES):
        ac = a[:, c * LANES:(c + 1) * LANES]
        lane = lax.broadcasted_iota(jnp.int32, ac.shape, 1)
        nxt = pltpu.roll(ac, LANES - 16, 1)
        prv = pltpu.roll(ac, 16, 1)
        parts.append(jnp.where((lane & 31) < 16, nxt, prv))
    partner = parts[0] if len(parts) == 1 else jnp.concatenate(parts, axis=1)
    return a * cos + partner * sin


def _chunk_scans(x):
    n = x.shape[0]
    pos = lax.broadcasted_iota(jnp.int32, x.shape, 0) & (CHUNK - 1)
    fwd, rev = x, x
    s = 1
    while s < CHUNK:
        fwd = fwd + jnp.where(pos >= s, pltpu.roll(fwd, s, 0), 0.0)
        rev = rev + jnp.where(pos + s < CHUNK, pltpu.roll(rev, n - s, 0), 0.0)
        s *= 2
    return fwd, rev


def _gate_maps(raw, alog, dtb):
    lane = lax.broadcasted_iota(jnp.int32, raw.shape, 1)
    beta = jax.nn.sigmoid(raw)
    y = raw + dtb
    softplus = jnp.maximum(y, 0.0) + jnp.log(1.0 + jnp.exp(-jnp.abs(y)))
    g = -jnp.exp(alog) * softplus
    fwd, rev = _chunk_scans(g)
    gc = jnp.where(lane < N_GATE + DN_HEADS, fwd, rev)
    total = fwd + rev - g
    e_gc = pltpu.roll(jnp.exp(gc), N_GATE, 1)
    e_dec = pltpu.roll(jnp.exp(total - gc), 2 * N_GATE, 1)
    e_tot = pltpu.roll(jnp.exp(total), 3 * N_GATE, 1)
    out = jnp.where(lane < N_GATE, beta,
                    jnp.where(lane < 2 * N_GATE, gc,
                              jnp.where(lane < 3 * N_GATE, e_gc,
                                        jnp.where(lane < 4 * N_GATE, e_dec, e_tot))))
    return out, gc


def _conv_silu_norm(win_ref, w_ref, o_ref, n_norm):
    tm = o_ref.shape[0]
    for hd in range(o_ref.shape[1] // LANES):
        cols = slice(hd * LANES, (hd + 1) * LANES)
        win = win_ref[:, cols]
        acc = None
        for j in range(CONV_K):
            shift = (CONV_K // 2 - j) % win.shape[0]
            tap = win if shift == 0 else pltpu.roll(win, shift, 0)
            term = tap[HALO:HALO + tm] * w_ref[j:j + 1, cols]
            acc = term if acc is None else acc + term
        y = _silu(acc)
        if hd < n_norm:
            y = y * lax.rsqrt(jnp.sum(y * y, axis=-1, keepdims=True) + NORM_EPS)
        o_ref[:, cols] = y


def _inproj_kernel(is_ctx, x_ref, xp_ref, xn_ref, mod_ref, n1w_ref, w_ref, qnw_ref, knw_ref, cos_ref,
                   sin_ref, gmat_ref, alog_ref, dtb_ref, cw_ref, *refs):
    outs, win_ref = refs[:-1], refs[-1]
    i = pl.program_id(1)
    nt = pl.num_programs(1)
    tm = x_ref.shape[0]

    def norm_mod(x):
        y = x * lax.rsqrt(jnp.mean(x * x, axis=-1, keepdims=True) + NORM_EPS) * n1w_ref[...]
        return (y * (1.0 + mod_ref[1:2, :]) + mod_ref[0:1, :]).astype(BF16)

    h = norm_mod(x_ref[...])

    def proj(lo, hi):
        return jnp.dot(h, w_ref[:, lo:hi], preferred_element_type=F32)

    conv_lo = OFF_DQKV + (DN_W if is_ctx else 0)
    edge = jnp.dot(norm_mod(jnp.concatenate([xp_ref[...], xn_ref[...]], axis=0)), w_ref[:, conv_lo:OFF_Z],
                   preferred_element_type=F32)
    win_ref[0:HALO, :] = jnp.where(i > 0, edge[:HALO], 0.0)
    win_ref[HALO:HALO + tm, :] = proj(conv_lo, OFF_Z)
    win_ref[HALO + tm:, :] = jnp.where(i < nt - 1, edge[HALO:], 0.0)

    kk = proj(OFF_K, OFF_V)
    kk = kk * lax.rsqrt(_group_mean_sq(kk, gmat_ref[:ATTN_KV_W, :ATTN_KV_W], ATTN_HEAD_DIM)
                        + NORM_EPS) * knw_ref[...]
    vv = proj(OFF_V, OFF_DQKV)
    gates, gc = _gate_maps(proj(OFF_GATE, IN_COLS_PAD), alog_ref[...], dtb_ref[...])
    if is_ctx:
        k_out, v_out, conv_out, gate_out, gt_out = outs
    else:
        q_out, k_out, v_out, conv_out, sz_out, gate_out, gt_out = outs
        cos = cos_ref[...]
        sin = sin_ref[...]
        kk = _rope(kk, cos, sin)
        qq = proj(OFF_Q, OFF_K)
        qq = qq * lax.rsqrt(_group_mean_sq(qq, gmat_ref[...], ATTN_HEAD_DIM) + NORM_EPS) * qnw_ref[...]
        rep = ATTN_W // LANES
        qq = _rope(qq, jnp.concatenate([cos] * rep, axis=1), jnp.concatenate([sin] * rep, axis=1))
        qb = (qq * (ATTN_HEAD_DIM ** -0.5)).astype(q_out.dtype)
        zeros = jnp.zeros((qb.shape[0], ATTN_HEAD_DIM), q_out.dtype)
        for hd in range(ATTN_HEADS):
            seg = qb[:, hd * ATTN_HEAD_DIM:(hd + 1) * ATTN_HEAD_DIM]
            pair = [seg, zeros] if hd // ATTN_GROUP == 0 else [zeros, seg]
            q_out[:, hd * LANES:(hd + 1) * LANES] = jnp.concatenate(pair, axis=1)
        sz_out[...] = _silu(proj(OFF_Z, OFF_GATE))
    _conv_silu_norm(win_ref, cw_ref, conv_out, DN_HEADS if is_ctx else 2 * DN_HEADS)
    k_out[...] = kk.astype(k_out.dtype)
    v_out[:, :ATTN_KV_W] = vv.astype(v_out.dtype)
    v_out[:, ATTN_KV_W:] = jnp.ones((vv.shape[0], V_EXT_W - ATTN_KV_W), v_out.dtype)
    gate_out[...] = gates
    gc_t = gc.T
    for ch in range(gt_out.shape[0]):
        gt_out[ch] = gc_t[N_GATE:2 * N_GATE, ch * CHUNK:(ch + 1) * CHUNK]


def _inproj(x, mod, mod_index, is_ctx, tm, consts):
    b, t, d = x.shape
    n1w, w_in, qnw, knw, cos, sin, gmat, alog, dtb, conv_w = consts
    nt = t // tm
    per = tm // HALO
    last = t // HALO - 1
    row = lambda bi, i: (bi, i, 0)
    full = lambda bi, i: (0, 0)
    conv_cols = conv_w.shape[1]
    in_specs = [
        pl.BlockSpec((None, tm, d), row),
        pl.BlockSpec((None, HALO, d), lambda bi, i: (bi, jnp.maximum(i * per - 1, 0), 0)),
        pl.BlockSpec((None, HALO, d), lambda bi, i: (bi, jnp.minimum((i + 1) * per, last), 0)),
        pl.BlockSpec((None, N_MOD, d), mod_index),
        pl.BlockSpec((1, d), full),
        pl.BlockSpec(w_in.shape, full),
        pl.BlockSpec(qnw.shape, full),
        pl.BlockSpec(knw.shape, full),
        pl.BlockSpec((tm, LANES), lambda bi, i: (i, 0)),
        pl.BlockSpec((tm, LANES), lambda bi, i: (i, 0)),
        pl.BlockSpec(gmat.shape, full),
        pl.BlockSpec((1, GATE_W), full),
        pl.BlockSpec((1, GATE_W), full),
        pl.BlockSpec(conv_w.shape, full),
    ]

    def out(width, dtype):
        return pl.BlockSpec((None, tm, width), row), jax.ShapeDtypeStruct((b, t, width), dtype)

    if is_ctx:
        outs = [out(ATTN_KV_W, BF16), out(V_EXT_W, BF16), out(conv_cols, F32), out(GATE_W, F32)]
    else:
        outs = [out(Q_EXT_W, BF16), out(ATTN_KV_W, BF16), out(V_EXT_W, BF16),
                out(conv_cols, F32), out(DN_W, F32), out(GATE_W, F32)]
    outs.append((pl.BlockSpec((None, tm // CHUNK, N_GATE, CHUNK), lambda bi, i: (bi, i, 0, 0)),
                 jax.ShapeDtypeStruct((b, t // CHUNK, N_GATE, CHUNK), F32)))
    return pl.pallas_call(
        functools.partial(_inproj_kernel, is_ctx),
        grid=(b, nt),
        in_specs=in_specs,
        out_specs=[o[0] for o in outs],
        out_shape=[o[1] for o in outs],
        scratch_shapes=[pltpu.VMEM((tm + 2 * HALO, conv_cols), F32)],
        compiler_params=_cparams(("parallel", "parallel")),
        name="inproj_ctx" if is_ctx else "inproj",
    )(x, x, x, mod, n1w, w_in, qnw, knw, cos, sin, gmat, alog, dtb, conv_w)


def _attn_kernel(online, q_ref, kc_ref, vc_ref, k_ref, v_ref, o_ref, bad_ref, m_ref, acc_ref):
    ki = pl.program_id(2)
    nk = pl.num_programs(2)

    def block(kb, vb, first):
        reps = kb.shape[0] // LANES

        def scores(h):
            return lax.dot_general(q_ref[:, h * LANES:(h + 1) * LANES], kb, (((1,), (1,)), ((), ())),
                                   preferred_element_type=F32)

        s_next = scores(0)
        for h in range(ATTN_HEADS):
            s = s_next
            if h + 1 < ATTN_HEADS:
                s_next = scores(h + 1)
            if first:
                m = jnp.broadcast_to(jnp.max(s, axis=-1, keepdims=True), m_ref.shape[1:])
                m_ref[h] = m
            elif online:
                m_prev = m_ref[h]
                m = jnp.maximum(m_prev, jnp.max(s, axis=-1, keepdims=True))
                alpha = jnp.exp(m_prev - m)
                m_ref[h] = m
            else:
                m = m_ref[h]
            p = jnp.exp(s - jnp.concatenate([m] * reps, axis=1))
            pv = jnp.dot(p.astype(BF16), vb, preferred_element_type=F32)
            if first:
                acc_ref[h] = pv
            elif online:
                acc_ref[h] = jnp.concatenate([alpha] * (V_EXT_W // LANES), axis=1) * acc_ref[h] + pv
            else:
                acc_ref[h] += pv

    @pl.when(ki == 0)
    def _():
        block(kc_ref[...], vc_ref[...], True)

    block(k_ref[...], v_ref[...], False)

    @pl.when(ki == nk - 1)
    def _():
        bad = jnp.zeros((1, 1), F32)
        for h in range(ATTN_HEADS):
            j = h // ATTN_GROUP
            acc = acc_ref[h]
            num = acc[:, j * ATTN_HEAD_DIM:(j + 1) * ATTN_HEAD_DIM]
            den = acc[:, ATTN_KV_W + j * ATTN_HEAD_DIM:ATTN_KV_W + (j + 1) * ATTN_HEAD_DIM]
            out = num / den
            o_ref[:, h * ATTN_HEAD_DIM:(h + 1) * ATTN_HEAD_DIM] = out.astype(o_ref.dtype)
            bad = jnp.maximum(bad, jnp.max(jnp.where(jnp.isfinite(out), 0.0, 1.0), keepdims=True))
        bad_ref[...] = jnp.broadcast_to(bad, bad_ref.shape)


def _attention(online, q, k, v, kc, vc, tq, tk):
    b, t, _ = q.shape
    n_ctx = kc.shape[1]
    nq = t // tq
    return pl.pallas_call(
        functools.partial(_attn_kernel, online),
        grid=(b, nq, t // tk),
        in_specs=[
            pl.BlockSpec((None, tq, Q_EXT_W), lambda bi, qi, ki: (bi, qi, 0)),
            pl.BlockSpec((None, n_ctx, ATTN_KV_W), lambda bi, qi, ki: (bi, 0, 0)),
            pl.BlockSpec((None, n_ctx, V_EXT_W), lambda bi, qi, ki: (bi, 0, 0)),
            pl.BlockSpec((None, tk, ATTN_KV_W), lambda bi, qi, ki: (bi, ki, 0)),
            pl.BlockSpec((None, tk, V_EXT_W), lambda bi, qi, ki: (bi, ki, 0)),
        ],
        out_specs=[pl.BlockSpec((None, tq, ATTN_W), lambda bi, qi, ki: (bi, qi, 0)),
                   pl.BlockSpec((None, None, 8, LANES), lambda bi, qi, ki: (bi, qi, 0, 0))],
        out_shape=[jax.ShapeDtypeStruct((b, t, ATTN_W), BF16),
                   jax.ShapeDtypeStruct((b, nq, 8, LANES), F32)],
        scratch_shapes=[pltpu.VMEM((ATTN_HEADS, tq, LANES), F32),
                        pltpu.VMEM((ATTN_HEADS, tq, V_EXT_W), F32)],
        compiler_params=_cparams(("parallel", "parallel", "arbitrary")),
        name="attention_online" if online else "attention",
    )(q, kc, vc, k, v)


def _dot_nt(a, b):
    return lax.dot_general(a, b, (((1,), (1,)), ((), ())), preferred_element_type=F32)


def _dot_tn(a, b):
    return lax.dot_general(a, b, (((0,), (0,)), ((), ())), preferred_element_type=F32)


def _unit_tri_inverses(ns, row, col):
    eye = (row == col).astype(F32)
    ds = None
    s = 1
    while s < CHUNK:
        sh = s.bit_length() - 1
        join = ((row >> (sh + 1)) == (col >> (sh + 1))) & ((row >> sh) != (col >> sh))
        es = [jnp.where(join, n, 0.0) for n in ns]
        if s == 1:
            ds = [eye - e for e in es]
        else:
            eds = [_dot_bf16(e, d) for e, d in zip(es, ds)]
            ds = [d - _dot_bf16(d, ed) for d, ed in zip(ds, eds)]
        s *= 2
    return ds


def _gdn_kernel(has_q, *refs):
    if has_q:
        (xf_ref, xb_ref, gf_ref, gb_ref, tf_ref, tb_ref, s0_ref, of_ref, ob_ref, sfin_ref, s_ref) = refs
    else:
        (xf_ref, xb_ref, gf_ref, gb_ref, tf_ref, tb_ref, s0_ref, sfin_ref, s_ref) = refs
    i = pl.program_id(0)
    nc = pl.num_programs(0)
    nb = xf_ref.shape[0]

    @pl.when(i == 0)
    def _():
        s_ref[...] = s0_ref[...]

    row = lax.broadcasted_iota(jnp.int32, (CHUNK, CHUNK), 0)
    col = lax.broadcasted_iota(jnp.int32, (CHUNK, CHUNK), 1)
    off_k = DN_W if has_q else 0
    off_v = off_k + DN_W
    scale = DN_HEAD_DIM ** -0.5
    units = [(bi, d, h) for bi in range(nb) for d in range(N_DIR) for h in range(DN_HEADS)]

    ops = []
    for bi, d, h in units:
        x_ref, g_ref, t_ref = (xf_ref, gf_ref, tf_ref) if d == 0 else (xb_ref, gb_ref, tb_ref)
        incl = (row >= col) if d == 0 else (row <= col)
        strict = (row > col) if d == 0 else (row < col)
        u = d * DN_HEADS + h

        def gate_col(kind):
            return g_ref[bi, :, kind * N_GATE + u:kind * N_GATE + u + 1]

        beta, gc, e_gc, e_dec = gate_col(0), gate_col(1), gate_col(2), gate_col(3)
        e_tot = g_ref[bi, 0:1, 4 * N_GATE + u:4 * N_GATE + u + 1]
        gc_t = t_ref[bi, u:u + 1, :]
        head = lambda off: x_ref[bi, :, off + h * DN_HEAD_DIM:off + (h + 1) * DN_HEAD_DIM]
        k, v = head(off_k), head(off_v)
        k16 = k.astype(BF16)
        kb = k * beta
        decay = jnp.exp(jnp.where(incl, gc - gc_t, -jnp.inf))
        if has_q:
            q = head(0) * scale
            kq = _dot_nt(jnp.concatenate([kb, q], axis=0).astype(BF16), k16)
            kk, qk = kq[:CHUNK], kq[CHUNK:]
            intra = (qk * decay).astype(BF16)
            qd = (q * e_gc).astype(BF16)
        else:
            kk = _dot_nt(kb.astype(BF16), k16)
            intra = qd = None
        ops.append(dict(
            n=jnp.where(strict, kk * decay, 0.0),
            rhs=jnp.concatenate([v * beta, kb * e_gc], axis=1).astype(BF16),
            k_dec=(k * e_dec).astype(BF16), e_tot=e_tot, intra=intra, qd=qd))

    t_invs = _unit_tri_inverses([o["n"] for o in ops], row, col)
    uws = [jnp.dot(t.astype(BF16), o["rhs"], preferred_element_type=F32) for t, o in zip(t_invs, ops)]

    reads = []
    for idx, (o, uw) in enumerate(zip(ops, uws)):
        s_old = s_ref[idx]
        w = uw[:, DN_HEAD_DIM:].astype(BF16)
        lhs = jnp.concatenate([w, o["qd"]], axis=0) if has_q else w
        reads.append((s_old, jnp.dot(lhs, s_old.astype(BF16), preferred_element_type=F32)))

    for idx, ((bi, d, h), o, uw, (s_old, rd)) in enumerate(zip(units, ops, uws, reads)):
        v_new = (uw[:, :DN_HEAD_DIM] - rd[:CHUNK]).astype(BF16)
        s_ref[idx] = s_old * o["e_tot"] + _dot_tn(o["k_dec"], v_new)
        if has_q:
            out = rd[CHUNK:] + jnp.dot(o["intra"], v_new, preferred_element_type=F32)
            o_ref = of_ref if d == 0 else ob_ref
            o_ref[bi, :, h * DN_HEAD_DIM:(h + 1) * DN_HEAD_DIM] = out

    @pl.when(i == nc - 1)
    def _():
        sfin_ref[...] = s_ref[...]


def _gdn_scan(x, gates, gates_t, s0, has_q):
    b, t, c = x.shape
    nc = t // CHUNK
    n_units = b * N_GATE
    fwd = lambda i: (0, i, 0)
    bwd = lambda i: (0, nc - 1 - i, 0)
    state_spec = pl.BlockSpec((n_units, DN_HEAD_DIM, DN_HEAD_DIM), lambda i: (0, 0, 0))
    state_shape = jax.ShapeDtypeStruct((n_units, DN_HEAD_DIM, DN_HEAD_DIM), F32)
    out_specs = [state_spec]
    out_shape = [state_shape]
    if has_q:
        o_shape = jax.ShapeDtypeStruct((b, t, DN_W), F32)
        out_specs = [pl.BlockSpec((b, CHUNK, DN_W), fwd), pl.BlockSpec((b, CHUNK, DN_W), bwd)] + out_specs
        out_shape = [o_shape, o_shape] + out_shape
    return pl.pallas_call(
        functools.partial(_gdn_kernel, has_q),
        grid=(nc,),
        in_specs=[pl.BlockSpec((b, CHUNK, c), fwd), pl.BlockSpec((b, CHUNK, c), bwd),
                  pl.BlockSpec((b, CHUNK, GATE_W), fwd), pl.BlockSpec((b, CHUNK, GATE_W), bwd),
                  pl.BlockSpec((b, None, N_GATE, CHUNK), lambda i: (0, i, 0, 0)),
                  pl.BlockSpec((b, None, N_GATE, CHUNK), lambda i: (0, nc - 1 - i, 0, 0)),
                  state_spec],
        out_specs=out_specs,
        out_shape=out_shape,
        scratch_shapes=[pltpu.VMEM((n_units, DN_HEAD_DIM, DN_HEAD_DIM), F32)],
        compiler_params=_cparams(("arbitrary",)),
        name="gdn_scan" if has_q else "gdn_scan_ctx",
    )(x, x, gates, gates, gates_t, gates_t, s0)


def _out_mlp_kernel(ff_tile, x_ref, oa_ref, of_ref, ob_ref, sz_ref, mod_ref, dnw_ref, wo_ref, n2w_ref,
                    w1_ref, w2_ref, o_ref):
    o_dn = of_ref[...] + ob_ref[...]
    parts = [oa_ref[...]]
    for h in range(DN_HEADS):
        cols = slice(h * DN_HEAD_DIM, (h + 1) * DN_HEAD_DIM)
        seg = o_dn[:, cols]
        seg = seg * lax.rsqrt(jnp.mean(seg * seg, axis=-1, keepdims=True) + NORM_EPS) * dnw_ref[...]
        parts.append((seg * sz_ref[:, cols]).astype(BF16))
    mixed = jnp.dot(jnp.concatenate(parts, axis=1), wo_ref[...], preferred_element_type=F32)
    x = x_ref[...] + mod_ref[2:3, :] * mixed

    y = x * lax.rsqrt(jnp.mean(x * x, axis=-1, keepdims=True) + NORM_EPS) * n2w_ref[...]
    h2 = (y * (1.0 + mod_ref[4:5, :]) + mod_ref[3:4, :]).astype(BF16)
    acc = None
    for c in range(w1_ref.shape[1] // ff_tile):
        a = jnp.dot(h2, w1_ref[:, c * ff_tile:(c + 1) * ff_tile], preferred_element_type=F32)
        a = jnp.square(jnp.maximum(a, 0.0)).astype(BF16)
        part = jnp.dot(a, w2_ref[c * ff_tile:(c + 1) * ff_tile, :], preferred_element_type=F32)
        acc = part if acc is None else acc + part
    o_ref[...] = x + mod_ref[5:6, :] * acc


def _out_mlp(x, o_attn, o_f, o_b, sz, mod, dnw, w_out, n2w, w1, w2, tm, ff_tile):
    b, t, d = x.shape
    row = lambda bi, i: (bi, i, 0)
    full = lambda bi, i: (0, 0)
    resident = lambda w: pl.BlockSpec(w.shape, full, pipeline_mode=pl.Buffered(1))
    return pl.pallas_call(
        functools.partial(_out_mlp_kernel, ff_tile),
        grid=(b, t // tm),
        in_specs=[pl.BlockSpec((None, tm, d), row),
                  pl.BlockSpec((None, tm, ATTN_W), row),
                  pl.BlockSpec((None, tm, DN_W), row),
                  pl.BlockSpec((None, tm, DN_W), row),
                  pl.BlockSpec((None, tm, DN_W), row),
                  pl.BlockSpec((None, N_MOD, d), lambda bi, i: (bi, 0, 0)),
                  pl.BlockSpec((1, DN_HEAD_DIM), full),
                  resident(w_out),
                  pl.BlockSpec((1, d), full),
                  resident(w1),
                  resident(w2)],
        out_specs=pl.BlockSpec((None, tm, d), row),
        out_shape=jax.ShapeDtypeStruct((b, t, d), F32),
        compiler_params=_cparams(("parallel", "parallel")),
        name="out_mlp",
    )(x, o_attn, o_f, o_b, sz, mod, dnw, w_out, n2w, w1, w2)


def _rope_tables(t):
    half = ATTN_HEAD_DIM // 4
    inv = ROPE_THETA ** (-jnp.arange(half, dtype=F32) / half)
    pos = jnp.arange(t, dtype=jnp.int32)
    ang_r = (pos // GRID_W).astype(F32)[:, None] * inv[None, :]
    ang_c = (pos % GRID_W).astype(F32)[:, None] * inv[None, :]
    cos = jnp.concatenate([jnp.cos(ang_r)] * 2 + [jnp.cos(ang_c)] * 2, axis=1)
    sin = jnp.concatenate([-jnp.sin(ang_r), jnp.sin(ang_r), -jnp.sin(ang_c), jnp.sin(ang_c)], axis=1)
    rep = LANES // ATTN_HEAD_DIM
    return jnp.tile(cos, (1, rep)), jnp.tile(sin, (1, rep))


def _pad_lanes(vec, offset):
    return jnp.zeros((1, GATE_W), F32).at[0, offset:offset + vec.size].set(vec.reshape(-1).astype(F32))


def kernel(x, c, ctx, c_ctx, w_mod, b_mod, norm1_w, w_in, q_norm_w, k_norm_w, conv_w, a_log, dt_bias,
           dn_norm_w, w_out, norm2_w, w_mlp1, w_mlp2):
    b, t, d = x.shape
    n_ctx = ctx.shape[1]
    depth = w_mod.shape[0]
    cos, sin = _rope_tables(t)
    head_id = jnp.arange(ATTN_W, dtype=jnp.int32) // ATTN_HEAD_DIM
    gmat = (head_id[:, None] == head_id[None, :]).astype(BF16)
    c_rows = jnp.zeros((8, d), F32).at[:b].set(c).at[b].set(c_ctx)
    tm = min(512, t)
    for l in range(depth):
        mod = _modulation(c_rows, w_mod[l], b_mod[l]).reshape(8, N_MOD, d)
        w_in_p = jnp.pad(w_in[l], ((0, 0), (0, IN_COLS_PAD - w_in.shape[2]))).astype(BF16)
        consts = (norm1_w[l].reshape(1, d), w_in_p,
                  jnp.tile(q_norm_w[l], ATTN_HEADS).reshape(1, ATTN_W),
                  jnp.tile(k_norm_w[l], ATTN_KV_HEADS).reshape(1, ATTN_KV_W),
                  cos, sin, gmat, _pad_lanes(a_log[l], N_GATE), _pad_lanes(dt_bias[l], N_GATE))
        q_a, k_a, v_a, qkv, sz, gates, gates_t = _inproj(
            x, mod, lambda bi, i: (bi, 0, 0), False, tm, consts + (conv_w[l],))
        k_c, v_c, ckv, cgates, cgates_t = _inproj(
            ctx, mod, lambda bi, i: (b, 0, 0), True, n_ctx, consts + (conv_w[l][:, DN_W:],))

        attn_args = (q_a, k_a, v_a, k_c, v_c, min(1024, t), min(1024, t))
        o_fixed, bad = _attention(False, *attn_args)
        o_attn = lax.cond(jnp.max(bad) > 0.0, lambda: _attention(True, *attn_args)[0], lambda: o_fixed)

        s_zero = jnp.zeros((b * N_GATE, DN_HEAD_DIM, DN_HEAD_DIM), F32)
        (s_ctx,) = _gdn_scan(ckv, cgates, cgates_t, s_zero, False)
        o_f, o_b, _ = _gdn_scan(qkv, gates, gates_t, s_ctx, True)

        x = _out_mlp(x, o_attn, o_f, o_b, sz, mod, dn_norm_w[l].reshape(1, DN_HEAD_DIM),
                     w_out[l].astype(BF16), norm2_w[l].reshape(1, d), w_mlp1[l].astype(BF16),
                     w_mlp2[l].astype(BF16), tm, 1024)
    return x
```

```python
import functools

import jax
import jax.numpy as jnp
from jax import lax
from jax.experimental import pallas as pl
from jax.experimental.pallas import tpu as pltpu

F32 = jnp.float32
BF16 = jnp.bfloat16

GRID_W = 64
ATTN_HEAD_DIM = 64
ATTN_HEADS = 8
ATTN_KV_HEADS = 2
ATTN_GROUP = ATTN_HEADS // ATTN_KV_HEADS
ATTN_W = ATTN_HEADS * ATTN_HEAD_DIM
ATTN_KV_W = ATTN_KV_HEADS * ATTN_HEAD_DIM
DN_HEAD_DIM = 128
DN_HEADS = 4
DN_W = DN_HEADS * DN_HEAD_DIM
N_DIR = 2
CONV_K = 5
CHUNK = 64
N_MOD = 6
ROPE_THETA = 10000.0
NORM_EPS = 1e-6
LANES = 128
GATE_W = LANES
N_GATE = N_DIR * DN_HEADS

OFF_Q = 0
OFF_K = OFF_Q + ATTN_W
OFF_V = OFF_K + ATTN_KV_W
OFF_DQKV = OFF_V + ATTN_KV_W
OFF_Z = OFF_DQKV + 3 * DN_W
OFF_GATE = OFF_Z + DN_W
IN_COLS_PAD = OFF_GATE + GATE_W

HALO = 8
Q_EXT_W = ATTN_HEADS * LANES
V_EXT_W = 2 * LANES

VMEM_LIMIT = 56 * 1024 * 1024


def _cparams(sem):
    return pltpu.CompilerParams(dimension_semantics=sem, vmem_limit_bytes=VMEM_LIMIT)


def _dot_bf16(a, b):
    return jnp.dot(a.astype(BF16), b.astype(BF16), preferred_element_type=F32)


def _dot_f32(a, b):
    return jnp.dot(a, b, preferred_element_type=F32, precision=lax.Precision.HIGHEST)


def _split_bf16(a):
    hi = a.astype(BF16)
    lo = (a - hi.astype(F32)).astype(BF16)
    return hi, lo


def _silu(x):
    return x * jax.nn.sigmoid(x)


def _mod_kernel(c_ref, w_ref, b_ref, o_ref):
    o_ref[...] = _dot_f32(_silu(c_ref[...]), w_ref[...]) + b_ref[...]


def _modulation(c_rows, w_mod, b_mod, layer):
    r, d = c_rows.shape
    n = w_mod.shape[2]
    tn = 1024
    return pl.pallas_call(
        _mod_kernel,
        grid=(n // tn,),
        in_specs=[pl.BlockSpec((r, d), lambda j: (0, 0)),
                  pl.BlockSpec((None, d, tn), lambda j: (layer, 0, j)),
                  pl.BlockSpec((1, tn), lambda j: (0, j))],
        out_specs=pl.BlockSpec((r, tn), lambda j: (0, j)),
        out_shape=jax.ShapeDtypeStruct((r, n), F32),
        compiler_params=_cparams(("arbitrary",)),
        name="modulation",
    )(c_rows, w_mod, b_mod.reshape(1, n))


def _group_mean_sq(a, gmat, group):
    hi, lo = _split_bf16(a * a)
    return jnp.dot(jnp.concatenate([hi, lo], axis=1), jnp.concatenate([gmat, gmat], axis=0),
                   preferred_element_type=F32) * (1.0 / group)


def _rope(a, cos, sin):
    parts = []
    for c in range(a.shape[1] // LANES):
        ac = a[:, c * LANES:(c + 1) * LANES]
        lane = lax.broadcasted_iota(jnp.int32, ac.shape, 1)
        nxt = pltpu.roll(ac, LANES - 16, 1)
        prv = pltpu.roll(ac, 16, 1)
        parts.append(jnp.where((lane & 31) < 16, nxt, prv))
    partner = parts[0] if len(parts) == 1 else jnp.concatenate(parts, axis=1)
    return a * cos + partner * sin


def _chunk_scans(x):
    n = x.shape[0]
    pos = lax.broadcasted_iota(jnp.int32, x.shape, 0) & (CHUNK - 1)
    fwd, rev = x, x
    s = 1
    while s < CHUNK:
        fwd = fwd + jnp.where(pos >= s, pltpu.roll(fwd, s, 0), 0.0)
        rev = rev + jnp.where(pos + s < CHUNK, pltpu.roll(rev, n - s, 0), 0.0)
        s *= 2
    return fwd, rev


def _gate_maps(raw, alog, dtb):
    lane = lax.broadcasted_iota(jnp.int32, raw.shape, 1)
    beta = jax.nn.sigmoid(raw)
    y = raw + dtb
    softplus = jnp.maximum(y, 0.0) + jnp.log(1.0 + jnp.exp(-jnp.abs(y)))
    g = -jnp.exp(alog) * softplus
    fwd, rev = _chunk_scans(g)
    gc = jnp.where(lane < N_GATE + DN_HEADS, fwd, rev)
    total = fwd + rev - g
    e_gc = pltpu.roll(jnp.exp(gc), N_GATE, 1)
    e_dec = pltpu.roll(jnp.exp(total - gc), 2 * N_GATE, 1)
    e_tot = pltpu.roll(jnp.exp(total), 3 * N_GATE, 1)
    out = jnp.where(lane < N_GATE, beta,
                    jnp.where(lane < 2 * N_GATE, gc,
                              jnp.where(lane < 3 * N_GATE, e_gc,
                                        jnp.where(lane < 4 * N_GATE, e_dec, e_tot))))
    return out, gc


def _conv_silu_norm(win_ref, w_ref, o_ref, n_norm, groups=None):
    tm = o_ref.shape[0]
    for hd in (range(o_ref.shape[1] // LANES) if groups is None else groups):
        cols = slice(hd * LANES, (hd + 1) * LANES)
        win = win_ref[:, cols]
        acc = None
        for j in range(CONV_K):
            shift = (CONV_K // 2 - j) % win.shape[0]
            tap = win if shift == 0 else pltpu.roll(win, shift, 0)
            term = tap[HALO:HALO + tm] * w_ref[j:j + 1, cols]
            acc = term if acc is None else acc + term
        y = _silu(acc)
        if hd < n_norm:
            y = y * lax.rsqrt(jnp.sum(y * y, axis=-1, keepdims=True) + NORM_EPS)
        o_ref[:, cols] = y


def _inproj_kernel(is_ctx, x_ref, mod_ref, n1w_ref, w_ref, qnw_ref, knw_ref, cos_ref, sin_ref,
                   gmat_ref, alog_ref, dtb_ref, cw_ref, *refs):
    x = x_ref[...]
    tm = x.shape[0]
    y = x * lax.rsqrt(jnp.mean(x * x, axis=-1, keepdims=True) + NORM_EPS) * n1w_ref[...]
    h = (y * (1.0 + mod_ref[1:2, :]) + mod_ref[0:1, :]).astype(BF16)

    def proj(lo, hi):
        return jnp.dot(h, w_ref[:, lo:hi], preferred_element_type=F32)

    kk = proj(OFF_K, OFF_V)
    kk = kk * lax.rsqrt(_group_mean_sq(kk, gmat_ref[:ATTN_KV_W, :ATTN_KV_W], ATTN_HEAD_DIM)
                        + NORM_EPS) * knw_ref[...]
    vv = proj(OFF_V, OFF_DQKV)
    gates, gc = _gate_maps(proj(OFF_GATE, IN_COLS_PAD), alog_ref[...], dtb_ref[...])
    if is_ctx:
        k_out, v_out, conv_out, gate_out, gt_out, win_ref = refs
        edge = jnp.zeros((HALO, conv_out.shape[1]), F32)
        win_ref[0:HALO, :] = edge
        win_ref[HALO:HALO + tm, :] = proj(OFF_DQKV + DN_W, OFF_Z)
        win_ref[HALO + tm:, :] = edge
        _conv_silu_norm(win_ref, cw_ref, conv_out, DN_HEADS)
    else:
        q_out, k_out, v_out, dqkv_out, sz_out, gate_out, gt_out = refs
        dqkv_out[...] = proj(OFF_DQKV, OFF_Z)
        cos = cos_ref[...]
        sin = sin_ref[...]
        kk = _rope(kk, cos, sin)
        qq = proj(OFF_Q, OFF_K)
        qq = qq * lax.rsqrt(_group_mean_sq(qq, gmat_ref[...], ATTN_HEAD_DIM) + NORM_EPS) * qnw_ref[...]
        rep = ATTN_W // LANES
        qq = _rope(qq, jnp.concatenate([cos] * rep, axis=1), jnp.concatenate([sin] * rep, axis=1))
        qb = (qq * (ATTN_HEAD_DIM ** -0.5)).astype(q_out.dtype)
        zeros = jnp.zeros((qb.shape[0], ATTN_HEAD_DIM), q_out.dtype)
        for hd in range(ATTN_HEADS):
            seg = qb[:, hd * ATTN_HEAD_DIM:(hd + 1) * ATTN_HEAD_DIM]
            pair = [seg, zeros] if hd // ATTN_GROUP == 0 else [zeros, seg]
            q_out[:, hd * LANES:(hd + 1) * LANES] = jnp.concatenate(pair, axis=1)
        sz_out[...] = _silu(proj(OFF_Z, OFF_GATE))
    k_out[...] = kk.astype(k_out.dtype)
    v_out[:, :ATTN_KV_W] = vv.astype(v_out.dtype)
    v_out[:, ATTN_KV_W:] = jnp.ones((vv.shape[0], V_EXT_W - ATTN_KV_W), v_out.dtype)
    gate_out[...] = gates
    gc_t = gc.T
    for ch in range(gt_out.shape[0]):
        gt_out[ch] = gc_t[N_GATE:2 * N_GATE, ch * CHUNK:(ch + 1) * CHUNK]


def _inproj(x, mod, mod_index, is_ctx, tm, consts):
    b, t, d = x.shape
    n1w, w_in, qnw, knw, cos, sin, gmat, alog, dtb, conv_w = consts
    nt = t // tm
    assert not is_ctx or nt == 1, "the context call convolves in-kernel and needs the whole sequence in one tile"
    row = lambda bi, i: (bi, i, 0)
    full = lambda bi, i: (0, 0)
    conv_cols = conv_w.shape[1]
    in_specs = [
        pl.BlockSpec((None, tm, d), row),
        pl.BlockSpec((None, N_MOD, d), mod_index),
        pl.BlockSpec((1, d), full),
        pl.BlockSpec(w_in.shape, full),
        pl.BlockSpec(qnw.shape, full),
        pl.BlockSpec(knw.shape, full),
        pl.BlockSpec((tm, LANES), lambda bi, i: (i, 0)),
        pl.BlockSpec((tm, LANES), lambda bi, i: (i, 0)),
        pl.BlockSpec(gmat.shape, full),
        pl.BlockSpec((1, GATE_W), full),
        pl.BlockSpec((1, GATE_W), full),
        pl.BlockSpec(conv_w.shape, full),
    ]

    def out(width, dtype):
        return pl.BlockSpec((None, tm, width), row), jax.ShapeDtypeStruct((b, t, width), dtype)

    if is_ctx:
        outs = [out(ATTN_KV_W, BF16), out(V_EXT_W, BF16), out(conv_cols, F32), out(GATE_W, F32)]
    else:
        outs = [out(Q_EXT_W, BF16), out(ATTN_KV_W, BF16), out(V_EXT_W, BF16),
                out(conv_cols, F32), out(DN_W, F32), out(GATE_W, F32)]
    outs.append((pl.BlockSpec((None, tm // CHUNK, N_GATE, CHUNK), lambda bi, i: (bi, i, 0, 0)),
                 jax.ShapeDtypeStruct((b, t // CHUNK, N_GATE, CHUNK), F32)))
    return pl.pallas_call(
        functools.partial(_inproj_kernel, is_ctx),
        grid=(b, nt),
        in_specs=in_specs,
        out_specs=[o[0] for o in outs],
        out_shape=[o[1] for o in outs],
        scratch_shapes=[pltpu.VMEM((tm + 2 * HALO, conv_cols), F32)] if is_ctx else [],
        compiler_params=_cparams(("parallel", "parallel")),
        name="inproj_ctx" if is_ctx else "inproj",
    )(x, mod, n1w, w_in, qnw, knw, cos, sin, gmat, alog, dtb, conv_w)


def _attn_kernel(online, q_ref, kc_ref, vc_ref, k_ref, v_ref, *refs):
    if online:
        o_ref, bad_ref, m_ref, acc_ref = refs
    else:
        d_ref, dp_ref, dn_ref, cw_ref, o_ref, bad_ref, conv_ref, m_ref, acc_ref, win_ref = refs
    ki = pl.program_id(2)
    nk = pl.num_programs(2)

    def block(kb, vb, first, side_job=None):
        reps = kb.shape[0] // LANES

        def scores(h):
            return lax.dot_general(q_ref[:, h * LANES:(h + 1) * LANES], kb, (((1,), (1,)), ((), ())),
                                   preferred_element_type=F32)

        s_next = scores(0)
        for h in range(ATTN_HEADS):
            s = s_next
            if h + 1 < ATTN_HEADS:
                s_next = scores(h + 1)
            if first:
                m = jnp.broadcast_to(jnp.max(s, axis=-1, keepdims=True), m_ref.shape[1:])
                m_ref[h] = m
            elif online:
                m_prev = m_ref[h]
                m = jnp.maximum(m_prev, jnp.max(s, axis=-1, keepdims=True))
                alpha = jnp.exp(m_prev - m)
                m_ref[h] = m
            else:
                m = m_ref[h]
            p = jnp.exp(s - jnp.concatenate([m] * reps, axis=1))
            pv = jnp.dot(p.astype(BF16), vb, preferred_element_type=F32)
            if first:
                acc_ref[h] = pv
            elif online:
                acc_ref[h] = jnp.concatenate([alpha] * (V_EXT_W // LANES), axis=1) * acc_ref[h] + pv
            else:
                acc_ref[h] += pv
            if side_job is not None:
                side_job(h)

    @pl.when(ki == 0)
    def _():
        block(kc_ref[...], vc_ref[...], True)

    if online:
        block(k_ref[...], v_ref[...], False)
    else:
        tile = pl.program_id(1) * nk + ki
        n_tiles = pl.num_programs(1) * nk
        ct = d_ref.shape[0]
        win_ref[0:HALO, :] = jnp.where(tile > 0, dp_ref[...], 0.0)
        win_ref[HALO:HALO + ct, :] = d_ref[...]
        win_ref[HALO + ct:, :] = jnp.where(tile < n_tiles - 1, dn_ref[...], 0.0)
        n_groups = conv_ref.shape[1] // LANES

        def conv_share(h):
            groups = range(h * n_groups // ATTN_HEADS, (h + 1) * n_groups // ATTN_HEADS)
            _conv_silu_norm(win_ref, cw_ref, conv_ref, 2 * DN_HEADS, groups)

        block(k_ref[...], v_ref[...], False, conv_share)

    @pl.when(ki == nk - 1)
    def _():
        bad = jnp.zeros((1, 1), F32)
        for h in range(ATTN_HEADS):
            j = h // ATTN_GROUP
            acc = acc_ref[h]
            num = acc[:, j * ATTN_HEAD_DIM:(j + 1) * ATTN_HEAD_DIM]
            den = acc[:, ATTN_KV_W + j * ATTN_HEAD_DIM:ATTN_KV_W + (j + 1) * ATTN_HEAD_DIM]
            out = num / den
            o_ref[:, h * ATTN_HEAD_DIM:(h + 1) * ATTN_HEAD_DIM] = out.astype(o_ref.dtype)
            bad = jnp.maximum(bad, jnp.max(jnp.where(jnp.isfinite(out), 0.0, 1.0), keepdims=True))
        bad_ref[...] = jnp.broadcast_to(bad, bad_ref.shape)


def _attention(online, q, k, v, kc, vc, tq, tk, dqkv=None, conv_w=None):
    b, t, _ = q.shape
    n_ctx = kc.shape[1]
    nq, nk = t // tq, t // tk
    in_specs = [
        pl.BlockSpec((None, tq, Q_EXT_W), lambda bi, qi, ki: (bi, qi, 0)),
        pl.BlockSpec((None, n_ctx, ATTN_KV_W), lambda bi, qi, ki: (bi, 0, 0)),
        pl.BlockSpec((None, n_ctx, V_EXT_W), lambda bi, qi, ki: (bi, 0, 0)),
        pl.BlockSpec((None, tk, ATTN_KV_W), lambda bi, qi, ki: (bi, ki, 0)),
        pl.BlockSpec((None, tk, V_EXT_W), lambda bi, qi, ki: (bi, ki, 0)),
    ]
    out_specs = [pl.BlockSpec((None, tq, ATTN_W), lambda bi, qi, ki: (bi, qi, 0)),
                 pl.BlockSpec((None, None, 8, LANES), lambda bi, qi, ki: (bi, qi, 0, 0))]
    out_shape = [jax.ShapeDtypeStruct((b, t, ATTN_W), BF16),
                 jax.ShapeDtypeStruct((b, nq, 8, LANES), F32)]
    scratch = [pltpu.VMEM((ATTN_HEADS, tq, LANES), F32),
               pltpu.VMEM((ATTN_HEADS, tq, V_EXT_W), F32)]
    args = (q, kc, vc, k, v)
    if not online:
        c = dqkv.shape[2]
        ct = t // (nq * nk)
        assert ct % HALO == 0 and ct * nq * nk == t
        per = ct // HALO
        last = t // HALO - 1
        tile = lambda qi, ki: qi * nk + ki
        in_specs += [
            pl.BlockSpec((None, ct, c), lambda bi, qi, ki: (bi, tile(qi, ki), 0)),
            pl.BlockSpec((None, HALO, c), lambda bi, qi, ki: (bi, jnp.maximum(tile(qi, ki) * per - 1, 0), 0)),
            pl.BlockSpec((None, HALO, c), lambda bi, qi, ki: (bi, jnp.minimum((tile(qi, ki) + 1) * per, last), 0)),
            pl.BlockSpec(conv_w.shape, lambda bi, qi, ki: (0, 0)),
        ]
        out_specs.append(pl.BlockSpec((None, ct, c), lambda bi, qi, ki: (bi, tile(qi, ki), 0)))
        out_shape.append(jax.ShapeDtypeStruct((b, t, c), F32))
        scratch.append(pltpu.VMEM((ct + 2 * HALO, c), F32))
        args += (dqkv, dqkv, dqkv, conv_w)
    return pl.pallas_call(
        functools.partial(_attn_kernel, online),
        grid=(b, nq, nk),
        in_specs=in_specs,
        out_specs=out_specs,
        out_shape=out_shape,
        scratch_shapes=scratch,
        compiler_params=_cparams(("parallel", "parallel", "arbitrary")),
        name="attention_online" if online else "attention",
    )(*args)


def _dot_nt(a, b):
    return lax.dot_general(a, b, (((1,), (1,)), ((), ())), preferred_element_type=F32)


def _dot_tn(a, b):
    return lax.dot_general(a, b, (((0,), (0,)), ((), ())), preferred_element_type=F32)


def _unit_tri_inverses(ns, row, col):
    eye = (row == col).astype(F32)
    ds = None
    s = 1
    while s < CHUNK:
        sh = s.bit_length() - 1
        join = ((row >> (sh + 1)) == (col >> (sh + 1))) & ((row >> sh) != (col >> sh))
        es = [jnp.where(join, n, 0.0) for n in ns]
        if s == 1:
            ds = [eye - e for e in es]
        else:
            eds = [_dot_bf16(e, d) for e, d in zip(es, ds)]
            ds = [d - _dot_bf16(d, ed) for d, ed in zip(ds, eds)]
        s *= 2
    return ds


def _gdn_kernel(has_q, *refs):
    if has_q:
        (xf_ref, xb_ref, gf_ref, gb_ref, tf_ref, tb_ref, s0_ref, of_ref, ob_ref, sfin_ref, s_ref) = refs
    else:
        (xf_ref, xb_ref, gf_ref, gb_ref, tf_ref, tb_ref, s0_ref, sfin_ref, s_ref) = refs
    i = pl.program_id(0)
    nc = pl.num_programs(0)
    nb = xf_ref.shape[0]

    @pl.when(i == 0)
    def _():
        s_ref[...] = s0_ref[...]

    row = lax.broadcasted_iota(jnp.int32, (CHUNK, CHUNK), 0)
    col = lax.broadcasted_iota(jnp.int32, (CHUNK, CHUNK), 1)
    off_k = DN_W if has_q else 0
    off_v = off_k + DN_W
    scale = DN_HEAD_DIM ** -0.5
    units = [(bi, d, h) for bi in range(nb) for d in range(N_DIR) for h in range(DN_HEADS)]

    ops = []
    for bi, d, h in units:
        x_ref, g_ref, t_ref = (xf_ref, gf_ref, tf_ref) if d == 0 else (xb_ref, gb_ref, tb_ref)
        incl = (row >= col) if d == 0 else (row <= col)
        strict = (row > col) if d == 0 else (row < col)
        u = d * DN_HEADS + h

        def gate_col(kind):
            return g_ref[bi, :, kind * N_GATE + u:kind * N_GATE + u + 1]

        beta, gc, e_gc, e_dec = gate_col(0), gate_col(1), gate_col(2), gate_col(3)
        e_tot = g_ref[bi, 0:1, 4 * N_GATE + u:4 * N_GATE + u + 1]
        gc_t = t_ref[bi, u:u + 1, :]
        head = lambda off: x_ref[bi, :, off + h * DN_HEAD_DIM:off + (h + 1) * DN_HEAD_DIM]
        k, v = head(off_k), head(off_v)
        k16 = k.astype(BF16)
        kb = k * beta
        decay = jnp.exp(jnp.where(incl, gc - gc_t, -jnp.inf))
        if has_q:
            q = head(0) * scale
            kq = _dot_nt(jnp.concatenate([kb, q], axis=0).astype(BF16), k16)
            kk, qk = kq[:CHUNK], kq[CHUNK:]
            intra = (qk * decay).astype(BF16)
            qd = (q * e_gc).astype(BF16)
        else:
            kk = _dot_nt(kb.astype(BF16), k16)
            intra = qd = None
        ops.append(dict(
            n=jnp.where(strict, kk * decay, 0.0),
            rhs=jnp.concatenate([v * beta, kb * e_gc], axis=1).astype(BF16),
            k_dec=(k * e_dec).astype(BF16), e_tot=e_tot, intra=intra, qd=qd))

    t_invs = _unit_tri_inverses([o["n"] for o in ops], row, col)
    uws = [jnp.dot(t.astype(BF16), o["rhs"], preferred_element_type=F32) for t, o in zip(t_invs, ops)]

    reads = []
    for idx, (o, uw) in enumerate(zip(ops, uws)):
        s_old = s_ref[idx]
        w = uw[:, DN_HEAD_DIM:].astype(BF16)
        lhs = jnp.concatenate([w, o["qd"]], axis=0) if has_q else w
        reads.append((s_old, jnp.dot(lhs, s_old.astype(BF16), preferred_element_type=F32)))

    for idx, ((bi, d, h), o, uw, (s_old, rd)) in enumerate(zip(units, ops, uws, reads)):
        v_new = (uw[:, :DN_HEAD_DIM] - rd[:CHUNK]).astype(BF16)
        s_ref[idx] = s_old * o["e_tot"] + _dot_tn(o["k_dec"], v_new)
        if has_q:
            out = rd[CHUNK:] + jnp.dot(o["intra"], v_new, preferred_element_type=F32)
            o_ref = of_ref if d == 0 else ob_ref
            o_ref[bi, :, h * DN_HEAD_DIM:(h + 1) * DN_HEAD_DIM] = out

    @pl.when(i == nc - 1)
    def _():
        sfin_ref[...] = s_ref[...]


def _gdn_scan(x, gates, gates_t, s0, has_q):
    b, t, c = x.shape
    nc = t // CHUNK
    n_units = b * N_GATE
    fwd = lambda i: (0, i, 0)
    bwd = lambda i: (0, nc - 1 - i, 0)
    state_spec = pl.BlockSpec((n_units, DN_HEAD_DIM, DN_HEAD_DIM), lambda i: (0, 0, 0))
    state_shape = jax.ShapeDtypeStruct((n_units, DN_HEAD_DIM, DN_HEAD_DIM), F32)
    out_specs = [state_spec]
    out_shape = [state_shape]
    if has_q:
        o_shape = jax.ShapeDtypeStruct((b, t, DN_W), F32)
        out_specs = [pl.BlockSpec((b, CHUNK, DN_W), fwd), pl.BlockSpec((b, CHUNK, DN_W), bwd)] + out_specs
        out_shape = [o_shape, o_shape] + out_shape
    return pl.pallas_call(
        functools.partial(_gdn_kernel, has_q),
        grid=(nc,),
        in_specs=[pl.BlockSpec((b, CHUNK, c), fwd), pl.BlockSpec((b, CHUNK, c), bwd),
                  pl.BlockSpec((b, CHUNK, GATE_W), fwd), pl.BlockSpec((b, CHUNK, GATE_W), bwd),
                  pl.BlockSpec((b, None, N_GATE, CHUNK), lambda i: (0, i, 0, 0)),
                  pl.BlockSpec((b, None, N_GATE, CHUNK), lambda i: (0, nc - 1 - i, 0, 0)),
                  state_spec],
        out_specs=out_specs,
        out_shape=out_shape,
        scratch_shapes=[pltpu.VMEM((n_units, DN_HEAD_DIM, DN_HEAD_DIM), F32)],
        compiler_params=_cparams(("arbitrary",)),
        name="gdn_scan" if has_q else "gdn_scan_ctx",
    )(x, x, gates, gates, gates_t, gates_t, s0)


def _out_mlp_kernel(ff_tile, x_ref, oa_ref, of_ref, ob_ref, sz_ref, mod_ref, dnw_ref, wo_ref, n2w_ref,
                    w1_ref, w2_ref, o_ref):
    o_dn = of_ref[...] + ob_ref[...]
    parts = [oa_ref[...]]
    for h in range(DN_HEADS):
        cols = slice(h * DN_HEAD_DIM, (h + 1) * DN_HEAD_DIM)
        seg = o_dn[:, cols]
        seg = seg * lax.rsqrt(jnp.mean(seg * seg, axis=-1, keepdims=True) + NORM_EPS) * dnw_ref[...]
        parts.append((seg * sz_ref[:, cols]).astype(BF16))
    mixed = jnp.dot(jnp.concatenate(parts, axis=1), wo_ref[...], preferred_element_type=F32)
    x = x_ref[...] + mod_ref[2:3, :] * mixed

    y = x * lax.rsqrt(jnp.mean(x * x, axis=-1, keepdims=True) + NORM_EPS) * n2w_ref[...]
    h2 = (y * (1.0 + mod_ref[4:5, :]) + mod_ref[3:4, :]).astype(BF16)
    acc = None
    for c in range(w1_ref.shape[1] // ff_tile):
        a = jnp.dot(h2, w1_ref[:, c * ff_tile:(c + 1) * ff_tile], preferred_element_type=F32)
        a = jnp.square(jnp.maximum(a, 0.0)).astype(BF16)
        part = jnp.dot(a, w2_ref[c * ff_tile:(c + 1) * ff_tile, :], preferred_element_type=F32)
        acc = part if acc is None else acc + part
    o_ref[...] = x + mod_ref[5:6, :] * acc


def _out_mlp(x, o_attn, o_f, o_b, sz, mod, dnw, w_out, n2w, w1, w2, tm, ff_tile):
    b, t, d = x.shape
    row = lambda bi, i: (bi, i, 0)
    full = lambda bi, i: (0, 0)
    resident = lambda w: pl.BlockSpec(w.shape, full, pipeline_mode=pl.Buffered(1))
    return pl.pallas_call(
        functools.partial(_out_mlp_kernel, ff_tile),
        grid=(b, t // tm),
        in_specs=[pl.BlockSpec((None, tm, d), row),
                  pl.BlockSpec((None, tm, ATTN_W), row),
                  pl.BlockSpec((None, tm, DN_W), row),
                  pl.BlockSpec((None, tm, DN_W), row),
                  pl.BlockSpec((None, tm, DN_W), row),
                  pl.BlockSpec((None, N_MOD, d), lambda bi, i: (bi, 0, 0)),
                  pl.BlockSpec((1, DN_HEAD_DIM), full),
                  resident(w_out),
                  pl.BlockSpec((1, d), full),
                  resident(w1),
                  resident(w2)],
        out_specs=pl.BlockSpec((None, tm, d), row),
        out_shape=jax.ShapeDtypeStruct((b, t, d), F32),
        compiler_params=_cparams(("parallel", "parallel")),
        name="out_mlp",
    )(x, o_attn, o_f, o_b, sz, mod, dnw, w_out, n2w, w1, w2)


def _rope_tables(t):
    half = ATTN_HEAD_DIM // 4
    inv = ROPE_THETA ** (-jnp.arange(half, dtype=F32) / half)
    pos = jnp.arange(t, dtype=jnp.int32)
    ang_r = (pos // GRID_W).astype(F32)[:, None] * inv[None, :]
    ang_c = (pos % GRID_W).astype(F32)[:, None] * inv[None, :]
    cos = jnp.concatenate([jnp.cos(ang_r)] * 2 + [jnp.cos(ang_c)] * 2, axis=1)
    sin = jnp.concatenate([-jnp.sin(ang_r), jnp.sin(ang_r), -jnp.sin(ang_c), jnp.sin(ang_c)], axis=1)
    rep = LANES // ATTN_HEAD_DIM
    return jnp.tile(cos, (1, rep)), jnp.tile(sin, (1, rep))


def _pad_lanes(vec, offset):
    return jnp.zeros((1, GATE_W), F32).at[0, offset:offset + vec.size].set(vec.reshape(-1).astype(F32))


def kernel(x, c, ctx, c_ctx, w_mod, b_mod, norm1_w, w_in, q_norm_w, k_norm_w, conv_w, a_log, dt_bias,
           dn_norm_w, w_out, norm2_w, w_mlp1, w_mlp2):
    b, t, d = x.shape
    n_ctx = ctx.shape[1]
    depth = w_mod.shape[0]
    cos, sin = _rope_tables(t)
    head_id = jnp.arange(ATTN_W, dtype=jnp.int32) // ATTN_HEAD_DIM
    gmat = (head_id[:, None] == head_id[None, :]).astype(BF16)
    c_rows = jnp.zeros((8, d), F32).at[:b].set(c).at[b].set(c_ctx)
    tm = min(512, t)
    for l in range(depth):
        mod = _modulation(c_rows, w_mod, b_mod[l], l).reshape(8, N_MOD, d)
        w_in_p = jnp.pad(w_in[l], ((0, 0), (0, IN_COLS_PAD - w_in.shape[2]))).astype(BF16)
        consts = (norm1_w[l].reshape(1, d), w_in_p,
                  jnp.tile(q_norm_w[l], ATTN_HEADS).reshape(1, ATTN_W),
                  jnp.tile(k_norm_w[l], ATTN_KV_HEADS).reshape(1, ATTN_KV_W),
                  cos, sin, gmat, _pad_lanes(a_log[l], N_GATE), _pad_lanes(dt_bias[l], N_GATE))
        q_a, k_a, v_a, dqkv, sz, gates, gates_t = _inproj(
            x, mod, lambda bi, i: (bi, 0, 0), False, tm, consts + (conv_w[l],))
        k_c, v_c, ckv, cgates, cgates_t = _inproj(
            ctx, mod, lambda bi, i: (b, 0, 0), True, n_ctx, consts + (conv_w[l][:, DN_W:],))

        attn_args = (q_a, k_a, v_a, k_c, v_c, min(1024, t), min(1024, t))
        o_fixed, bad, qkv = _attention(False, *attn_args, dqkv=dqkv, conv_w=conv_w[l])
        o_attn = lax.cond(jnp.max(bad) > 0.0, lambda: _attention(True, *attn_args)[0], lambda: o_fixed)

        s_zero = jnp.zeros((b * N_GATE, DN_HEAD_DIM, DN_HEAD_DIM), F32)
        (s_ctx,) = _gdn_scan(ckv, cgates, cgates_t, s_zero, False)
        o_f, o_b, _ = _gdn_scan(qkv, gates, gates_t, s_ctx, True)

        x = _out_mlp(x, o_attn, o_f, o_b, sz, mod, dn_norm_w[l].reshape(1, DN_HEAD_DIM),
                     w_out[l].astype(BF16), norm2_w[l].reshape(1, d), w_mlp1[l].astype(BF16),
                     w_mlp2[l].astype(BF16), tm, 1024)
    return x
```

```python
import functools

import jax
import jax.numpy as jnp
from jax import lax
from jax.experimental import pallas as pl
from jax.experimental.pallas import tpu as pltpu

F32 = jnp.float32
BF16 = jnp.bfloat16

GRID_W = 64
ATTN_HEAD_DIM = 64
ATTN_HEADS = 8
ATTN_KV_HEADS = 2
ATTN_GROUP = ATTN_HEADS // ATTN_KV_HEADS
ATTN_W = ATTN_HEADS * ATTN_HEAD_DIM
ATTN_KV_W = ATTN_KV_HEADS * ATTN_HEAD_DIM
DN_HEAD_DIM = 128
DN_HEADS = 4
DN_W = DN_HEADS * DN_HEAD_DIM
N_DIR = 2
CONV_K = 5
CHUNK = 64
N_MOD = 6
ROPE_THETA = 10000.0
NORM_EPS = 1e-6
LANES = 128
GATE_W = LANES
N_GATE = N_DIR * DN_HEADS

OFF_Q = 0
OFF_K = OFF_Q + ATTN_W
OFF_V = OFF_K + ATTN_KV_W
OFF_DQKV = OFF_V + ATTN_KV_W
OFF_Z = OFF_DQKV + 3 * DN_W
OFF_GATE = OFF_Z + DN_W
IN_COLS_PAD = OFF_GATE + GATE_W

HALO = 8
Q_EXT_W = ATTN_HEADS * LANES
V_EXT_W = 2 * LANES

VMEM_LIMIT = 56 * 1024 * 1024


def _cparams(sem):
    return pltpu.CompilerParams(dimension_semantics=sem, vmem_limit_bytes=VMEM_LIMIT)


def _dot_bf16(a, b):
    return jnp.dot(a.astype(BF16), b.astype(BF16), preferred_element_type=F32)


def _dot_f32(a, b):
    return jnp.dot(a, b, preferred_element_type=F32, precision=lax.Precision.HIGHEST)


def _split_bf16(a):
    hi = a.astype(BF16)
    lo = (a - hi.astype(F32)).astype(BF16)
    return hi, lo


def _silu(x):
    return x * jax.nn.sigmoid(x)


def _mod_kernel(c_ref, w_ref, b_ref, o_ref):
    o_ref[...] = _dot_f32(_silu(c_ref[...]), w_ref[...]) + b_ref[...]


def _modulation(c_rows, w_mod, b_mod, layer):
    r, d = c_rows.shape
    n = w_mod.shape[2]
    tn = 1024
    return pl.pallas_call(
        _mod_kernel,
        grid=(n // tn,),
        in_specs=[pl.BlockSpec((r, d), lambda j: (0, 0)),
                  pl.BlockSpec((None, d, tn), lambda j: (layer, 0, j)),
                  pl.BlockSpec((1, tn), lambda j: (0, j))],
        out_specs=pl.BlockSpec((r, tn), lambda j: (0, j)),
        out_shape=jax.ShapeDtypeStruct((r, n), F32),
        compiler_params=_cparams(("arbitrary",)),
        name="modulation",
    )(c_rows, w_mod, b_mod.reshape(1, n))


def _group_mean_sq(a, gmat, group):
    hi, lo = _split_bf16(a * a)
    return jnp.dot(jnp.concatenate([hi, lo], axis=1), jnp.concatenate([gmat, gmat], axis=0),
                   preferred_element_type=F32) * (1.0 / group)


def _rope(a, cos, sin):
    parts = []
    for c in range(a.shape[1] // LANES):
        ac = a[:, c * LANES:(c + 1) * LANES]
        lane = lax.broadcasted_iota(jnp.int32, ac.shape, 1)
        nxt = pltpu.roll(ac, LANES - 16, 1)
        prv = pltpu.roll(ac, 16, 1)
        parts.append(jnp.where((lane & 31) < 16, nxt, prv))
    partner = parts[0] if len(parts) == 1 else jnp.concatenate(parts, axis=1)
    return a * cos + partner * sin


def _chunk_scans(x):
    n = x.shape[0]
    pos = lax.broadcasted_iota(jnp.int32, x.shape, 0) & (CHUNK - 1)
    fwd, rev = x, x
    s = 1
    while s < CHUNK:
        fwd = fwd + jnp.where(pos >= s, pltpu.roll(fwd, s, 0), 0.0)
        rev = rev + jnp.where(pos + s < CHUNK, pltpu.roll(rev, n - s, 0), 0.0)
        s *= 2
    return fwd, rev


def _gate_maps(raw, alog, dtb):
    lane = lax.broadcasted_iota(jnp.int32, raw.shape, 1)
    beta = jax.nn.sigmoid(raw)
    y = raw + dtb
    softplus = jnp.maximum(y, 0.0) + jnp.log(1.0 + jnp.exp(-jnp.abs(y)))
    g = -jnp.exp(alog) * softplus
    fwd, rev = _chunk_scans(g)
    gc = jnp.where(lane < N_GATE + DN_HEADS, fwd, rev)
    total = fwd + rev - g
    e_gc = pltpu.roll(jnp.exp(gc), N_GATE, 1)
    e_dec = pltpu.roll(jnp.exp(total - gc), 2 * N_GATE, 1)
    e_tot = pltpu.roll(jnp.exp(total), 3 * N_GATE, 1)
    out = jnp.where(lane < N_GATE, beta,
                    jnp.where(lane < 2 * N_GATE, gc,
                              jnp.where(lane < 3 * N_GATE, e_gc,
                                        jnp.where(lane < 4 * N_GATE, e_dec, e_tot))))
    return out, gc


def _conv_silu_norm(win_ref, w_ref, o_ref, n_norm, groups=None):
    tm = o_ref.shape[0]
    for hd in (range(o_ref.shape[1] // LANES) if groups is None else groups):
        cols = slice(hd * LANES, (hd + 1) * LANES)
        win = win_ref[:, cols]
        acc = None
        for j in range(CONV_K):
            shift = (CONV_K // 2 - j) % win.shape[0]
            tap = win if shift == 0 else pltpu.roll(win, shift, 0)
            term = tap[HALO:HALO + tm] * w_ref[j:j + 1, cols]
            acc = term if acc is None else acc + term
        y = _silu(acc)
        if hd < n_norm:
            y = y * lax.rsqrt(jnp.sum(y * y, axis=-1, keepdims=True) + NORM_EPS)
        o_ref[:, cols] = y


def _inproj_kernel(is_ctx, x_ref, mod_ref, n1w_ref, w_ref, qnw_ref, knw_ref, cos_ref, sin_ref,
                   gmat_ref, alog_ref, dtb_ref, cw_ref, *refs):
    x = x_ref[...]
    tm = x.shape[0]
    y = x * lax.rsqrt(jnp.mean(x * x, axis=-1, keepdims=True) + NORM_EPS) * n1w_ref[...]
    h = (y * (1.0 + mod_ref[1:2, :]) + mod_ref[0:1, :]).astype(BF16)

    def proj(lo, hi):
        return jnp.dot(h, w_ref[:, lo:hi], preferred_element_type=F32)

    kk = proj(OFF_K, OFF_V)
    kk = kk * lax.rsqrt(_group_mean_sq(kk, gmat_ref[:ATTN_KV_W, :ATTN_KV_W], ATTN_HEAD_DIM)
                        + NORM_EPS) * knw_ref[...]
    vv = proj(OFF_V, OFF_DQKV)
    gates, gc = _gate_maps(proj(OFF_GATE, IN_COLS_PAD), alog_ref[...], dtb_ref[...])
    if is_ctx:
        k_out, v_out, conv_out, gate_out, gt_out, win_ref = refs
        edge = jnp.zeros((HALO, conv_out.shape[1]), F32)
        win_ref[0:HALO, :] = edge
        win_ref[HALO:HALO + tm, :] = proj(OFF_DQKV + DN_W, OFF_Z)
        win_ref[HALO + tm:, :] = edge
        _conv_silu_norm(win_ref, cw_ref, conv_out, DN_HEADS)
    else:
        q_out, k_out, v_out, dqkv_out, sz_out, gate_out, gt_out = refs
        dqkv_out[...] = proj(OFF_DQKV, OFF_Z)
        cos = cos_ref[...]
        sin = sin_ref[...]
        kk = _rope(kk, cos, sin)
        qq = proj(OFF_Q, OFF_K)
        qq = qq * lax.rsqrt(_group_mean_sq(qq, gmat_ref[...], ATTN_HEAD_DIM) + NORM_EPS) * qnw_ref[...]
        rep = ATTN_W // LANES
        qq = _rope(qq, jnp.concatenate([cos] * rep, axis=1), jnp.concatenate([sin] * rep, axis=1))
        qb = (qq * (ATTN_HEAD_DIM ** -0.5)).astype(q_out.dtype)
        zeros = jnp.zeros((qb.shape[0], ATTN_HEAD_DIM), q_out.dtype)
        for hd in range(ATTN_HEADS):
            seg = qb[:, hd * ATTN_HEAD_DIM:(hd + 1) * ATTN_HEAD_DIM]
            pair = [seg, zeros] if hd // ATTN_GROUP == 0 else [zeros, seg]
            q_out[:, hd * LANES:(hd + 1) * LANES] = jnp.concatenate(pair, axis=1)
        sz_out[...] = _silu(proj(OFF_Z, OFF_GATE))
    k_out[...] = kk.astype(k_out.dtype)
    v_out[:, :ATTN_KV_W] = vv.astype(v_out.dtype)
    v_out[:, ATTN_KV_W:] = jnp.ones((vv.shape[0], V_EXT_W - ATTN_KV_W), v_out.dtype)
    gate_out[...] = gates
    gc_t = gc.T
    for ch in range(gt_out.shape[0]):
        gt_out[ch] = gc_t[N_GATE:2 * N_GATE, ch * CHUNK:(ch + 1) * CHUNK]


def _inproj(x, mod, mod_index, is_ctx, tm, consts):
    b, t, d = x.shape
    n1w, w_in, qnw, knw, cos, sin, gmat, alog, dtb, conv_w = consts
    nt = t // tm
    assert not is_ctx or nt == 1, "the context call convolves in-kernel and needs the whole sequence in one tile"
    row = lambda bi, i: (bi, i, 0)
    full = lambda bi, i: (0, 0)
    conv_cols = conv_w.shape[1]
    in_specs = [
        pl.BlockSpec((None, tm, d), row),
        pl.BlockSpec((None, N_MOD, d), mod_index),
        pl.BlockSpec((1, d), full),
        pl.BlockSpec(w_in.shape, full),
        pl.BlockSpec(qnw.shape, full),
        pl.BlockSpec(knw.shape, full),
        pl.BlockSpec((tm, LANES), lambda bi, i: (i, 0)),
        pl.BlockSpec((tm, LANES), lambda bi, i: (i, 0)),
        pl.BlockSpec(gmat.shape, full),
        pl.BlockSpec((1, GATE_W), full),
        pl.BlockSpec((1, GATE_W), full),
        pl.BlockSpec(conv_w.shape, full),
    ]

    def out(width, dtype):
        return pl.BlockSpec((None, tm, width), row), jax.ShapeDtypeStruct((b, t, width), dtype)

    if is_ctx:
        outs = [out(ATTN_KV_W, BF16), out(V_EXT_W, BF16), out(conv_cols, F32), out(GATE_W, F32)]
    else:
        outs = [out(Q_EXT_W, BF16), out(ATTN_KV_W, BF16), out(V_EXT_W, BF16),
                out(conv_cols, F32), out(DN_W, F32), out(GATE_W, F32)]
    outs.append((pl.BlockSpec((None, tm // CHUNK, N_GATE, CHUNK), lambda bi, i: (bi, i, 0, 0)),
                 jax.ShapeDtypeStruct((b, t // CHUNK, N_GATE, CHUNK), F32)))
    return pl.pallas_call(
        functools.partial(_inproj_kernel, is_ctx),
        grid=(b, nt),
        in_specs=in_specs,
        out_specs=[o[0] for o in outs],
        out_shape=[o[1] for o in outs],
        scratch_shapes=[pltpu.VMEM((tm + 2 * HALO, conv_cols), F32)] if is_ctx else [],
        compiler_params=_cparams(("parallel", "parallel")),
        name="inproj_ctx" if is_ctx else "inproj",
    )(x, mod, n1w, w_in, qnw, knw, cos, sin, gmat, alog, dtb, conv_w)


def _attn_kernel(online, q_ref, kc_ref, vc_ref, k_ref, v_ref, *refs):
    if online:
        o_ref, bad_ref, m_ref, acc_ref = refs
    else:
        d_ref, dp_ref, dn_ref, cw_ref, o_ref, bad_ref, conv_ref, m_ref, acc_ref, win_ref = refs
    ki = pl.program_id(2)
    nk = pl.num_programs(2)

    def block(kb, vb, first, side_job=None):
        reps = kb.shape[0] // LANES

        def scores(h):
            return lax.dot_general(q_ref[:, h * LANES:(h + 1) * LANES], kb, (((1,), (1,)), ((), ())),
                                   preferred_element_type=F32)

        s_next = scores(0)
        for h in range(ATTN_HEADS):
            s = s_next
            if h + 1 < ATTN_HEADS:
                s_next = scores(h + 1)
            if first:
                m = jnp.broadcast_to(jnp.max(s, axis=-1, keepdims=True), m_ref.shape[1:])
                m_ref[h] = m
            elif online:
                m_prev = m_ref[h]
                m = jnp.maximum(m_prev, jnp.max(s, axis=-1, keepdims=True))
                alpha = jnp.exp(m_prev - m)
                m_ref[h] = m
            else:
                m = m_ref[h]
            p = jnp.exp(s - jnp.concatenate([m] * reps, axis=1))
            pv = jnp.dot(p.astype(BF16), vb, preferred_element_type=F32)
            if first:
                acc_ref[h] = pv
            elif online:
                acc_ref[h] = jnp.concatenate([alpha] * (V_EXT_W // LANES), axis=1) * acc_ref[h] + pv
            else:
                acc_ref[h] += pv
            if side_job is not None:
                side_job(h)

    @pl.when(ki == 0)
    def _():
        block(kc_ref[...], vc_ref[...], True)

    if online:
        block(k_ref[...], v_ref[...], False)
    else:
        tile = pl.program_id(1) * nk + ki
        n_tiles = pl.num_programs(1) * nk
        ct = d_ref.shape[0]
        win_ref[0:HALO, :] = jnp.where(tile > 0, dp_ref[...], 0.0)
        win_ref[HALO:HALO + ct, :] = d_ref[...]
        win_ref[HALO + ct:, :] = jnp.where(tile < n_tiles - 1, dn_ref[...], 0.0)
        n_groups = conv_ref.shape[1] // LANES

        def conv_share(h):
            groups = range(h * n_groups // ATTN_HEADS, (h + 1) * n_groups // ATTN_HEADS)
            _conv_silu_norm(win_ref, cw_ref, conv_ref, 2 * DN_HEADS, groups)

        block(k_ref[...], v_ref[...], False, conv_share)

    @pl.when(ki == nk - 1)
    def _():
        bad = jnp.zeros((1, 1), F32)
        for h in range(ATTN_HEADS):
            j = h // ATTN_GROUP
            acc = acc_ref[h]
            num = acc[:, j * ATTN_HEAD_DIM:(j + 1) * ATTN_HEAD_DIM]
            den = acc[:, ATTN_KV_W + j * ATTN_HEAD_DIM:ATTN_KV_W + (j + 1) * ATTN_HEAD_DIM]
            out = num / den
            o_ref[:, h * ATTN_HEAD_DIM:(h + 1) * ATTN_HEAD_DIM] = out.astype(o_ref.dtype)
            bad = jnp.maximum(bad, jnp.max(jnp.where(jnp.isfinite(out), 0.0, 1.0), keepdims=True))
        bad_ref[...] = jnp.broadcast_to(bad, bad_ref.shape)


def _attention(online, q, k, v, kc, vc, tq, tk, dqkv=None, conv_w=None):
    b, t, _ = q.shape
    n_ctx = kc.shape[1]
    nq, nk = t // tq, t // tk
    in_specs = [
        pl.BlockSpec((None, tq, Q_EXT_W), lambda bi, qi, ki: (bi, qi, 0)),
        pl.BlockSpec((None, n_ctx, ATTN_KV_W), lambda bi, qi, ki: (bi, 0, 0)),
        pl.BlockSpec((None, n_ctx, V_EXT_W), lambda bi, qi, ki: (bi, 0, 0)),
        pl.BlockSpec((None, tk, ATTN_KV_W), lambda bi, qi, ki: (bi, ki, 0)),
        pl.BlockSpec((None, tk, V_EXT_W), lambda bi, qi, ki: (bi, ki, 0)),
    ]
    out_specs = [pl.BlockSpec((None, tq, ATTN_W), lambda bi, qi, ki: (bi, qi, 0)),
                 pl.BlockSpec((None, None, 8, LANES), lambda bi, qi, ki: (bi, qi, 0, 0))]
    out_shape = [jax.ShapeDtypeStruct((b, t, ATTN_W), BF16),
                 jax.ShapeDtypeStruct((b, nq, 8, LANES), F32)]
    scratch = [pltpu.VMEM((ATTN_HEADS, tq, LANES), F32),
               pltpu.VMEM((ATTN_HEADS, tq, V_EXT_W), F32)]
    args = (q, kc, vc, k, v)
    if not online:
        c = dqkv.shape[2]
        ct = t // (nq * nk)
        assert ct % HALO == 0 and ct * nq * nk == t
        per = ct // HALO
        last = t // HALO - 1
        tile = lambda qi, ki: qi * nk + ki
        in_specs += [
            pl.BlockSpec((None, ct, c), lambda bi, qi, ki: (bi, tile(qi, ki), 0)),
            pl.BlockSpec((None, HALO, c), lambda bi, qi, ki: (bi, jnp.maximum(tile(qi, ki) * per - 1, 0), 0)),
            pl.BlockSpec((None, HALO, c), lambda bi, qi, ki: (bi, jnp.minimum((tile(qi, ki) + 1) * per, last), 0)),
            pl.BlockSpec(conv_w.shape, lambda bi, qi, ki: (0, 0)),
        ]
        out_specs.append(pl.BlockSpec((None, ct, c), lambda bi, qi, ki: (bi, tile(qi, ki), 0)))
        out_shape.append(jax.ShapeDtypeStruct((b, t, c), F32))
        scratch.append(pltpu.VMEM((ct + 2 * HALO, c), F32))
        args += (dqkv, dqkv, dqkv, conv_w)
    return pl.pallas_call(
        functools.partial(_attn_kernel, online),
        grid=(b, nq, nk),
        in_specs=in_specs,
        out_specs=out_specs,
        out_shape=out_shape,
        scratch_shapes=scratch,
        compiler_params=_cparams(("parallel", "parallel", "arbitrary")),
        name="attention_online" if online else "attention",
    )(*args)


def _dot_nt(a, b):
    return lax.dot_general(a, b, (((1,), (1,)), ((), ())), preferred_element_type=F32)


def _dot_tn(a, b):
    return lax.dot_general(a, b, (((0,), (0,)), ((), ())), preferred_element_type=F32)


def _unit_tri_inverses(ns, row, col):
    eye = (row == col).astype(F32)
    ds = None
    s = 1
    while s < CHUNK:
        sh = s.bit_length() - 1
        join = ((row >> (sh + 1)) == (col >> (sh + 1))) & ((row >> sh) != (col >> sh))
        es = [jnp.where(join, n, 0.0) for n in ns]
        if s == 1:
            ds = [eye - e for e in es]
        else:
            eds = [_dot_bf16(e, d) for e, d in zip(es, ds)]
            ds = [d - _dot_bf16(d, ed) for d, ed in zip(ds, eds)]
        s *= 2
    return ds


def _gdn_kernel(has_q, *refs):
    if has_q:
        (xf_ref, xb_ref, gf_ref, gb_ref, tf_ref, tb_ref, s0_ref, of_ref, ob_ref, sfin_ref, s_ref) = refs
    else:
        (xf_ref, xb_ref, gf_ref, gb_ref, tf_ref, tb_ref, s0_ref, sfin_ref, s_ref) = refs
    i = pl.program_id(0)
    nc = pl.num_programs(0)
    nb = xf_ref.shape[0]

    @pl.when(i == 0)
    def _():
        s_ref[...] = s0_ref[...]

    row = lax.broadcasted_iota(jnp.int32, (CHUNK, CHUNK), 0)
    col = lax.broadcasted_iota(jnp.int32, (CHUNK, CHUNK), 1)
    off_k = DN_W if has_q else 0
    off_v = off_k + DN_W
    scale = DN_HEAD_DIM ** -0.5
    units = [(bi, d, h) for bi in range(nb) for d in range(N_DIR) for h in range(DN_HEADS)]

    ops = []
    for bi, d, h in units:
        x_ref, g_ref, t_ref = (xf_ref, gf_ref, tf_ref) if d == 0 else (xb_ref, gb_ref, tb_ref)
        incl = (row >= col) if d == 0 else (row <= col)
        strict = (row > col) if d == 0 else (row < col)
        u = d * DN_HEADS + h

        def gate_col(kind):
            return g_ref[bi, :, kind * N_GATE + u:kind * N_GATE + u + 1]

        beta, gc, e_gc, e_dec = gate_col(0), gate_col(1), gate_col(2), gate_col(3)
        e_tot = g_ref[bi, 0:1, 4 * N_GATE + u:4 * N_GATE + u + 1]
        gc_t = t_ref[bi, u:u + 1, :]
        head = lambda off: x_ref[bi, :, off + h * DN_HEAD_DIM:off + (h + 1) * DN_HEAD_DIM]
        k, v = head(off_k), head(off_v)
        k16 = k.astype(BF16)
        kb = k * beta
        decay = jnp.exp(jnp.where(incl, gc - gc_t, -jnp.inf))
        if has_q:
            q = head(0) * scale
            kq = _dot_nt(jnp.concatenate([kb, q], axis=0).astype(BF16), k16)
            kk, qk = kq[:CHUNK], kq[CHUNK:]
            intra = (qk * decay).astype(BF16)
            qd = (q * e_gc).astype(BF16)
        else:
            kk = _dot_nt(kb.astype(BF16), k16)
            intra = qd = None
        ops.append(dict(
            n=jnp.where(strict, kk * decay, 0.0),
            rhs=jnp.concatenate([v * beta, kb * e_gc], axis=1).astype(BF16),
            k_dec=(k * e_dec).astype(BF16), e_tot=e_tot, intra=intra, qd=qd))

    t_invs = _unit_tri_inverses([o["n"] for o in ops], row, col)
    uws = [jnp.dot(t.astype(BF16), o["rhs"], preferred_element_type=F32) for t, o in zip(t_invs, ops)]

    reads = []
    for idx, (o, uw) in enumerate(zip(ops, uws)):
        s_old = s_ref[idx]
        w = uw[:, DN_HEAD_DIM:].astype(BF16)
        lhs = jnp.concatenate([w, o["qd"]], axis=0) if has_q else w
        reads.append((s_old, jnp.dot(lhs, s_old.astype(BF16), preferred_element_type=F32)))

    for idx, ((bi, d, h), o, uw, (s_old, rd)) in enumerate(zip(units, ops, uws, reads)):
        v_new = (uw[:, :DN_HEAD_DIM] - rd[:CHUNK]).astype(BF16)
        s_ref[idx] = s_old * o["e_tot"] + _dot_tn(o["k_dec"], v_new)
        if has_q:
            out = rd[CHUNK:] + jnp.dot(o["intra"], v_new, preferred_element_type=F32)
            o_ref = of_ref if d == 0 else ob_ref
            o_ref[bi, :, h * DN_HEAD_DIM:(h + 1) * DN_HEAD_DIM] = out

    @pl.when(i == nc - 1)
    def _():
        sfin_ref[...] = s_ref[...]


def _gdn_scan(x, gates, gates_t, s0, has_q):
    b, t, c = x.shape
    nc = t // CHUNK
    n_units = b * N_GATE
    fwd = lambda i: (0, i, 0)
    bwd = lambda i: (0, nc - 1 - i, 0)
    state_spec = pl.BlockSpec((n_units, DN_HEAD_DIM, DN_HEAD_DIM), lambda i: (0, 0, 0))
    state_shape = jax.ShapeDtypeStruct((n_units, DN_HEAD_DIM, DN_HEAD_DIM), F32)
    out_specs = [state_spec]
    out_shape = [state_shape]
    if has_q:
        o_shape = jax.ShapeDtypeStruct((b, t, DN_W), F32)
        out_specs = [pl.BlockSpec((b, CHUNK, DN_W), fwd), pl.BlockSpec((b, CHUNK, DN_W), bwd)] + out_specs
        out_shape = [o_shape, o_shape] + out_shape
    return pl.pallas_call(
        functools.partial(_gdn_kernel, has_q),
        grid=(nc,),
        in_specs=[pl.BlockSpec((b, CHUNK, c), fwd), pl.BlockSpec((b, CHUNK, c), bwd),
                  pl.BlockSpec((b, CHUNK, GATE_W), fwd), pl.BlockSpec((b, CHUNK, GATE_W), bwd),
                  pl.BlockSpec((b, None, N_GATE, CHUNK), lambda i: (0, i, 0, 0)),
                  pl.BlockSpec((b, None, N_GATE, CHUNK), lambda i: (0, nc - 1 - i, 0, 0)),
                  state_spec],
        out_specs=out_specs,
        out_shape=out_shape,
        scratch_shapes=[pltpu.VMEM((n_units, DN_HEAD_DIM, DN_HEAD_DIM), F32)],
        compiler_params=_cparams(("arbitrary",)),
        name="gdn_scan" if has_q else "gdn_scan_ctx",
    )(x, x, gates, gates, gates_t, gates_t, s0)


def _out_mlp_kernel(ff_tile, x_ref, oa_ref, of_ref, ob_ref, sz_ref, mod_ref, dnw_ref, wo_ref, n2w_ref,
                    w1_ref, w2_ref, o_ref):
    o_dn = of_ref[...] + ob_ref[...]
    parts = [oa_ref[...]]
    for h in range(DN_HEADS):
        cols = slice(h * DN_HEAD_DIM, (h + 1) * DN_HEAD_DIM)
        seg = o_dn[:, cols]
        seg = seg * lax.rsqrt(jnp.mean(seg * seg, axis=-1, keepdims=True) + NORM_EPS) * dnw_ref[...]
        parts.append((seg * sz_ref[:, cols]).astype(BF16))
    mixed = jnp.dot(jnp.concatenate(parts, axis=1), wo_ref[...], preferred_element_type=F32)
    x = x_ref[...] + mod_ref[2:3, :] * mixed

    y = x * lax.rsqrt(jnp.mean(x * x, axis=-1, keepdims=True) + NORM_EPS) * n2w_ref[...]
    h2 = (y * (1.0 + mod_ref[4:5, :]) + mod_ref[3:4, :]).astype(BF16)
    acc = None
    for c in range(w1_ref.shape[1] // ff_tile):
        a = jnp.dot(h2, w1_ref[:, c * ff_tile:(c + 1) * ff_tile], preferred_element_type=F32)
        a = jnp.square(jnp.maximum(a, 0.0)).astype(BF16)
        part = jnp.dot(a, w2_ref[c * ff_tile:(c + 1) * ff_tile, :], preferred_element_type=F32)
        acc = part if acc is None else acc + part
    o_ref[...] = x + mod_ref[5:6, :] * acc


def _out_mlp(x, o_attn, o_f, o_b, sz, mod, dnw, w_out, n2w, w1, w2, tm, ff_tile):
    b, t, d = x.shape
    row = lambda bi, i: (bi, i, 0)
    full = lambda bi, i: (0, 0)
    resident = lambda w: pl.BlockSpec(w.shape, full, pipeline_mode=pl.Buffered(1))
    return pl.pallas_call(
        functools.partial(_out_mlp_kernel, ff_tile),
        grid=(b, t // tm),
        in_specs=[pl.BlockSpec((None, tm, d), row),
                  pl.BlockSpec((None, tm, ATTN_W), row),
                  pl.BlockSpec((None, tm, DN_W), row),
                  pl.BlockSpec((None, tm, DN_W), row),
                  pl.BlockSpec((None, tm, DN_W), row),
                  pl.BlockSpec((None, N_MOD, d), lambda bi, i: (bi, 0, 0)),
                  pl.BlockSpec((1, DN_HEAD_DIM), full),
                  resident(w_out),
                  pl.BlockSpec((1, d), full),
                  resident(w1),
                  resident(w2)],
        out_specs=pl.BlockSpec((None, tm, d), row),
        out_shape=jax.ShapeDtypeStruct((b, t, d), F32),
        compiler_params=_cparams(("parallel", "parallel")),
        name="out_mlp",
    )(x, o_attn, o_f, o_b, sz, mod, dnw, w_out, n2w, w1, w2)


def _rope_tables(t):
    half = ATTN_HEAD_DIM // 4
    inv = ROPE_THETA ** (-jnp.arange(half, dtype=F32) / half)
    pos = jnp.arange(t, dtype=jnp.int32)
    ang_r = (pos // GRID_W).astype(F32)[:, None] * inv[None, :]
    ang_c = (pos % GRID_W).astype(F32)[:, None] * inv[None, :]
    cos = jnp.concatenate([jnp.cos(ang_r)] * 2 + [jnp.cos(ang_c)] * 2, axis=1)
    sin = jnp.concatenate([-jnp.sin(ang_r), jnp.sin(ang_r), -jnp.sin(ang_c), jnp.sin(ang_c)], axis=1)
    rep = LANES // ATTN_HEAD_DIM
    return jnp.tile(cos, (1, rep)), jnp.tile(sin, (1, rep))


def _pad_lanes(vec, offset):
    return jnp.zeros((1, GATE_W), F32).at[0, offset:offset + vec.size].set(vec.reshape(-1).astype(F32))


def kernel(x, c, ctx, c_ctx, w_mod, b_mod, norm1_w, w_in, q_norm_w, k_norm_w, conv_w, a_log, dt_bias,
           dn_norm_w, w_out, norm2_w, w_mlp1, w_mlp2):
    b, t, d = x.shape
    n_ctx = ctx.shape[1]
    depth = w_mod.shape[0]
    cos, sin = _rope_tables(t)
    head_id = jnp.arange(ATTN_W, dtype=jnp.int32) // ATTN_HEAD_DIM
    gmat = (head_id[:, None] == head_id[None, :]).astype(BF16)
    c_rows = jnp.zeros((8, d), F32).at[:b].set(c).at[b].set(c_ctx)
    tm = min(512, t)
    for l in range(depth):
        mod = _modulation(c_rows, w_mod, b_mod[l], l).reshape(8, N_MOD, d)
        w_in_p = jnp.pad(w_in[l], ((0, 0), (0, IN_COLS_PAD - w_in.shape[2]))).astype(BF16)
        consts = (norm1_w[l].reshape(1, d), w_in_p,
                  jnp.tile(q_norm_w[l], ATTN_HEADS).reshape(1, ATTN_W),
                  jnp.tile(k_norm_w[l], ATTN_KV_HEADS).reshape(1, ATTN_KV_W),
                  cos, sin, gmat, _pad_lanes(a_log[l], N_GATE), _pad_lanes(dt_bias[l], N_GATE))
        q_a, k_a, v_a, dqkv, sz, gates, gates_t = _inproj(
            x, mod, lambda bi, i: (bi, 0, 0), False, tm, consts + (conv_w[l],))
        k_c, v_c, ckv, cgates, cgates_t = _inproj(
            ctx, mod, lambda bi, i: (b, 0, 0), True, n_ctx, consts + (conv_w[l][:, DN_W:],))

        attn_args = (q_a, k_a, v_a, k_c, v_c, min(1024, t), min(2048, t))
        o_fixed, bad, qkv = _attention(False, *attn_args, dqkv=dqkv, conv_w=conv_w[l])
        o_attn = lax.cond(jnp.max(bad) > 0.0, lambda: _attention(True, *attn_args)[0], lambda: o_fixed)

        s_zero = jnp.zeros((b * N_GATE, DN_HEAD_DIM, DN_HEAD_DIM), F32)
        (s_ctx,) = _gdn_scan(ckv, cgates, cgates_t, s_zero, False)
        o_f, o_b, _ = _gdn_scan(qkv, gates, gates_t, s_ctx, True)

        x = _out_mlp(x, o_attn, o_f, o_b, sz, mod, dn_norm_w[l].reshape(1, DN_HEAD_DIM),
                     w_out[l].astype(BF16), norm2_w[l].reshape(1, d), w_mlp1[l].astype(BF16),
                     w_mlp2[l].astype(BF16), tm, 1024)
    return x
```

```python
import functools

import jax
import jax.numpy as jnp
from jax import lax
from jax.experimental import pallas as pl
from jax.experimental.pallas import tpu as pltpu

F32 = jnp.float32
BF16 = jnp.bfloat16

GRID_W = 64
ATTN_HEAD_DIM = 64
ATTN_HEADS = 8
ATTN_KV_HEADS = 2
ATTN_GROUP = ATTN_HEADS // ATTN_KV_HEADS
ATTN_W = ATTN_HEADS * ATTN_HEAD_DIM
ATTN_KV_W = ATTN_KV_HEADS * ATTN_HEAD_DIM
DN_HEAD_DIM = 128
DN_HEADS = 4
DN_W = DN_HEADS * DN_HEAD_DIM
N_DIR = 2
CONV_K = 5
CHUNK = 64
N_MOD = 6
ROPE_THETA = 10000.0
NORM_EPS = 1e-6
LANES = 128
GATE_W = LANES
N_GATE = N_DIR * DN_HEADS

OFF_Q = 0
OFF_K = OFF_Q + ATTN_W
OFF_V = OFF_K + ATTN_KV_W
OFF_DQKV = OFF_V + ATTN_KV_W
OFF_Z = OFF_DQKV + 3 * DN_W
OFF_GATE = OFF_Z + DN_W
IN_COLS_PAD = OFF_GATE + GATE_W

HALO = 8
Q_EXT_W = ATTN_HEADS * LANES
V_EXT_W = 2 * LANES

VMEM_LIMIT = 56 * 1024 * 1024


def _cparams(sem):
    return pltpu.CompilerParams(dimension_semantics=sem, vmem_limit_bytes=VMEM_LIMIT)


def _dot_bf16(a, b):
    return jnp.dot(a.astype(BF16), b.astype(BF16), preferred_element_type=F32)


def _dot_f32(a, b):
    return jnp.dot(a, b, preferred_element_type=F32, precision=lax.Precision.HIGHEST)


def _split_bf16(a):
    hi = a.astype(BF16)
    lo = (a - hi.astype(F32)).astype(BF16)
    return hi, lo


def _silu(x):
    return x * jax.nn.sigmoid(x)


def _mod_kernel(c_ref, w_ref, b_ref, o_ref):
    o_ref[...] = _dot_f32(_silu(c_ref[...]), w_ref[...]) + b_ref[...]


def _modulation(c_rows, w_mod, b_mod, layer):
    r, d = c_rows.shape
    n = w_mod.shape[2]
    tn = 1024
    return pl.pallas_call(
        _mod_kernel,
        grid=(n // tn,),
        in_specs=[pl.BlockSpec((r, d), lambda j: (0, 0)),
                  pl.BlockSpec((None, d, tn), lambda j: (layer, 0, j)),
                  pl.BlockSpec((1, tn), lambda j: (0, j))],
        out_specs=pl.BlockSpec((r, tn), lambda j: (0, j)),
        out_shape=jax.ShapeDtypeStruct((r, n), F32),
        compiler_params=_cparams(("arbitrary",)),
        name="modulation",
    )(c_rows, w_mod, b_mod.reshape(1, n))


def _group_mean_sq(a, gmat, group):
    hi, lo = _split_bf16(a * a)
    return jnp.dot(jnp.concatenate([hi, lo], axis=1), jnp.concatenate([gmat, gmat], axis=0),
                   preferred_element_type=F32) * (1.0 / group)


def _rope(a, cos, sin):
    parts = []
    for c in range(a.shape[1] // LANES):
        ac = a[:, c * LANES:(c + 1) * LANES]
        lane = lax.broadcasted_iota(jnp.int32, ac.shape, 1)
        nxt = pltpu.roll(ac, LANES - 16, 1)
        prv = pltpu.roll(ac, 16, 1)
        parts.append(jnp.where((lane & 31) < 16, nxt, prv))
    partner = parts[0] if len(parts) == 1 else jnp.concatenate(parts, axis=1)
    return a * cos + partner * sin


def _chunk_scans(x):
    n = x.shape[0]
    pos = lax.broadcasted_iota(jnp.int32, x.shape, 0) & (CHUNK - 1)
    fwd, rev = x, x
    s = 1
    while s < CHUNK:
        fwd = fwd + jnp.where(pos >= s, pltpu.roll(fwd, s, 0), 0.0)
        rev = rev + jnp.where(pos + s < CHUNK, pltpu.roll(rev, n - s, 0), 0.0)
        s *= 2
    return fwd, rev


def _gate_maps(raw, alog, dtb):
    lane = lax.broadcasted_iota(jnp.int32, raw.shape, 1)
    beta = jax.nn.sigmoid(raw)
    y = raw + dtb
    softplus = jnp.maximum(y, 0.0) + jnp.log(1.0 + jnp.exp(-jnp.abs(y)))
    g = -jnp.exp(alog) * softplus
    fwd, rev = _chunk_scans(g)
    gc = jnp.where(lane < N_GATE + DN_HEADS, fwd, rev)
    total = fwd + rev - g
    e_gc = pltpu.roll(jnp.exp(gc), N_GATE, 1)
    e_dec = pltpu.roll(jnp.exp(total - gc), 2 * N_GATE, 1)
    e_tot = pltpu.roll(jnp.exp(total), 3 * N_GATE, 1)
    out = jnp.where(lane < N_GATE, beta,
                    jnp.where(lane < 2 * N_GATE, gc,
                              jnp.where(lane < 3 * N_GATE, e_gc,
                                        jnp.where(lane < 4 * N_GATE, e_dec, e_tot))))
    return out, gc


def _conv_silu_norm(win_ref, w_ref, o_ref, n_norm, groups=None):
    tm = o_ref.shape[0]
    for hd in (range(o_ref.shape[1] // LANES) if groups is None else groups):
        cols = slice(hd * LANES, (hd + 1) * LANES)
        win = win_ref[:, cols]
        acc = None
        for j in range(CONV_K):
            shift = (CONV_K // 2 - j) % win.shape[0]
            tap = win if shift == 0 else pltpu.roll(win, shift, 0)
            term = tap[HALO:HALO + tm] * w_ref[j:j + 1, cols]
            acc = term if acc is None else acc + term
        y = _silu(acc)
        if hd < n_norm:
            y = y * lax.rsqrt(jnp.sum(y * y, axis=-1, keepdims=True) + NORM_EPS)
        o_ref[:, cols] = y


def _inproj_kernel(is_ctx, x_ref, mod_ref, n1w_ref, w_ref, qnw_ref, knw_ref, cos_ref, sin_ref,
                   gmat_ref, alog_ref, dtb_ref, cw_ref, *refs):
    x = x_ref[...]
    tm = x.shape[0]
    y = x * lax.rsqrt(jnp.mean(x * x, axis=-1, keepdims=True) + NORM_EPS) * n1w_ref[...]
    h = (y * (1.0 + mod_ref[1:2, :]) + mod_ref[0:1, :]).astype(BF16)

    def proj(lo, hi):
        return jnp.dot(h, w_ref[:, lo:hi], preferred_element_type=F32)

    kv = proj(OFF_K, OFF_DQKV)
    kk = kv[:, :ATTN_KV_W]
    kk = kk * lax.rsqrt(_group_mean_sq(kk, gmat_ref[:ATTN_KV_W, :ATTN_KV_W], ATTN_HEAD_DIM)
                        + NORM_EPS) * knw_ref[...]
    vv = kv[:, ATTN_KV_W:]
    gates, gc = _gate_maps(proj(OFF_GATE, IN_COLS_PAD), alog_ref[...], dtb_ref[...])
    if is_ctx:
        k_out, v_out, conv_out, gate_out, gt_out, win_ref = refs
        edge = jnp.zeros((HALO, conv_out.shape[1]), F32)
        win_ref[0:HALO, :] = edge
        win_ref[HALO:HALO + tm, :] = proj(OFF_DQKV + DN_W, OFF_Z)
        win_ref[HALO + tm:, :] = edge
        _conv_silu_norm(win_ref, cw_ref, conv_out, DN_HEADS)
    else:
        q_out, k_out, v_out, dqkv_out, sz_out, gate_out, gt_out = refs
        dqkv_out[...] = proj(OFF_DQKV, OFF_Z)
        cos = cos_ref[...]
        sin = sin_ref[...]
        kk = _rope(kk, cos, sin)
        qq = proj(OFF_Q, OFF_K)
        qq = qq * lax.rsqrt(_group_mean_sq(qq, gmat_ref[...], ATTN_HEAD_DIM) + NORM_EPS) * qnw_ref[...]
        rep = ATTN_W // LANES
        qq = _rope(qq, jnp.concatenate([cos] * rep, axis=1), jnp.concatenate([sin] * rep, axis=1))
        qb = (qq * (ATTN_HEAD_DIM ** -0.5)).astype(q_out.dtype)
        zeros = jnp.zeros((qb.shape[0], ATTN_HEAD_DIM), q_out.dtype)
        for hd in range(ATTN_HEADS):
            seg = qb[:, hd * ATTN_HEAD_DIM:(hd + 1) * ATTN_HEAD_DIM]
            pair = [seg, zeros] if hd // ATTN_GROUP == 0 else [zeros, seg]
            q_out[:, hd * LANES:(hd + 1) * LANES] = jnp.concatenate(pair, axis=1)
        sz_out[...] = _silu(proj(OFF_Z, OFF_GATE))
    k_out[...] = kk.astype(k_out.dtype)
    v_out[:, :ATTN_KV_W] = vv.astype(v_out.dtype)
    v_out[:, ATTN_KV_W:] = jnp.ones((vv.shape[0], V_EXT_W - ATTN_KV_W), v_out.dtype)
    gate_out[...] = gates
    gc_t = gc.T
    for ch in range(gt_out.shape[0]):
        gt_out[ch] = gc_t[N_GATE:2 * N_GATE, ch * CHUNK:(ch + 1) * CHUNK]


def _inproj(x, mod, mod_index, is_ctx, tm, consts):
    b, t, d = x.shape
    n1w, w_in, qnw, knw, cos, sin, gmat, alog, dtb, conv_w = consts
    nt = t // tm
    assert not is_ctx or nt == 1, "the context call convolves in-kernel and needs the whole sequence in one tile"
    row = lambda bi, i: (bi, i, 0)
    full = lambda bi, i: (0, 0)
    conv_cols = conv_w.shape[1]
    in_specs = [
        pl.BlockSpec((None, tm, d), row),
        pl.BlockSpec((None, N_MOD, d), mod_index),
        pl.BlockSpec((1, d), full),
        pl.BlockSpec(w_in.shape, full),
        pl.BlockSpec(qnw.shape, full),
        pl.BlockSpec(knw.shape, full),
        pl.BlockSpec((tm, LANES), lambda bi, i: (i, 0)),
        pl.BlockSpec((tm, LANES), lambda bi, i: (i, 0)),
        pl.BlockSpec(gmat.shape, full),
        pl.BlockSpec((1, GATE_W), full),
        pl.BlockSpec((1, GATE_W), full),
        pl.BlockSpec(conv_w.shape, full),
    ]

    def out(width, dtype):
        return pl.BlockSpec((None, tm, width), row), jax.ShapeDtypeStruct((b, t, width), dtype)

    if is_ctx:
        outs = [out(ATTN_KV_W, BF16), out(V_EXT_W, BF16), out(conv_cols, F32), out(GATE_W, F32)]
    else:
        outs = [out(Q_EXT_W, BF16), out(ATTN_KV_W, BF16), out(V_EXT_W, BF16),
                out(conv_cols, F32), out(DN_W, F32), out(GATE_W, F32)]
    outs.append((pl.BlockSpec((None, tm // CHUNK, N_GATE, CHUNK), lambda bi, i: (bi, i, 0, 0)),
                 jax.ShapeDtypeStruct((b, t // CHUNK, N_GATE, CHUNK), F32)))
    return pl.pallas_call(
        functools.partial(_inproj_kernel, is_ctx),
        grid=(b, nt),
        in_specs=in_specs,
        out_specs=[o[0] for o in outs],
        out_shape=[o[1] for o in outs],
        scratch_shapes=[pltpu.VMEM((tm + 2 * HALO, conv_cols), F32)] if is_ctx else [],
        compiler_params=_cparams(("parallel", "parallel")),
        name="inproj_ctx" if is_ctx else "inproj",
    )(x, mod, n1w, w_in, qnw, knw, cos, sin, gmat, alog, dtb, conv_w)


def _attn_kernel(online, q_ref, kc_ref, vc_ref, k_ref, v_ref, *refs):
    if online:
        o_ref, bad_ref, m_ref, acc_ref = refs
    else:
        d_ref, dp_ref, dn_ref, cw_ref, o_ref, bad_ref, conv_ref, m_ref, acc_ref, win_ref = refs
    ki = pl.program_id(2)
    nk = pl.num_programs(2)

    def block(kb, vb, first, side_job=None):
        reps = kb.shape[0] // LANES

        def scores(h):
            return lax.dot_general(q_ref[:, h * LANES:(h + 1) * LANES], kb, (((1,), (1,)), ((), ())),
                                   preferred_element_type=F32)

        s_next = scores(0)
        for h in range(ATTN_HEADS):
            s = s_next
            if h + 1 < ATTN_HEADS:
                s_next = scores(h + 1)
            if first:
                m = jnp.broadcast_to(jnp.max(s, axis=-1, keepdims=True), m_ref.shape[1:])
                m_ref[h] = m
            elif online:
                m_prev = m_ref[h]
                m = jnp.maximum(m_prev, jnp.max(s, axis=-1, keepdims=True))
                alpha = jnp.exp(m_prev - m)
                m_ref[h] = m
            else:
                m = m_ref[h]
            p = jnp.exp(s - jnp.concatenate([m] * reps, axis=1))
            pv = jnp.dot(p.astype(BF16), vb, preferred_element_type=F32)
            if first:
                acc_ref[h] = pv
            elif online:
                acc_ref[h] = jnp.concatenate([alpha] * (V_EXT_W // LANES), axis=1) * acc_ref[h] + pv
            else:
                acc_ref[h] += pv
            if side_job is not None:
                side_job(h)

    @pl.when(ki == 0)
    def _():
        block(kc_ref[...], vc_ref[...], True)

    if online:
        block(k_ref[...], v_ref[...], False)
    else:
        tile = pl.program_id(1) * nk + ki
        n_tiles = pl.num_programs(1) * nk
        ct = d_ref.shape[0]
        win_ref[0:HALO, :] = jnp.where(tile > 0, dp_ref[...], 0.0)
        win_ref[HALO:HALO + ct, :] = d_ref[...]
        win_ref[HALO + ct:, :] = jnp.where(tile < n_tiles - 1, dn_ref[...], 0.0)
        n_groups = conv_ref.shape[1] // LANES

        def conv_share(h):
            groups = range(h * n_groups // ATTN_HEADS, (h + 1) * n_groups // ATTN_HEADS)
            _conv_silu_norm(win_ref, cw_ref, conv_ref, 2 * DN_HEADS, groups)

        block(k_ref[...], v_ref[...], False, conv_share)

    @pl.when(ki == nk - 1)
    def _():
        bad = jnp.zeros((1, 1), F32)
        for h in range(ATTN_HEADS):
            j = h // ATTN_GROUP
            acc = acc_ref[h]
            num = acc[:, j * ATTN_HEAD_DIM:(j + 1) * ATTN_HEAD_DIM]
            den = acc[:, ATTN_KV_W + j * ATTN_HEAD_DIM:ATTN_KV_W + (j + 1) * ATTN_HEAD_DIM]
            out = num / den
            o_ref[:, h * ATTN_HEAD_DIM:(h + 1) * ATTN_HEAD_DIM] = out.astype(o_ref.dtype)
            bad = jnp.maximum(bad, jnp.max(jnp.where(jnp.isfinite(out), 0.0, 1.0), keepdims=True))
        bad_ref[...] = jnp.broadcast_to(bad, bad_ref.shape)


def _attention(online, q, k, v, kc, vc, tq, tk, dqkv=None, conv_w=None):
    b, t, _ = q.shape
    n_ctx = kc.shape[1]
    nq, nk = t // tq, t // tk
    in_specs = [
        pl.BlockSpec((None, tq, Q_EXT_W), lambda bi, qi, ki: (bi, qi, 0)),
        pl.BlockSpec((None, n_ctx, ATTN_KV_W), lambda bi, qi, ki: (bi, 0, 0)),
        pl.BlockSpec((None, n_ctx, V_EXT_W), lambda bi, qi, ki: (bi, 0, 0)),
        pl.BlockSpec((None, tk, ATTN_KV_W), lambda bi, qi, ki: (bi, ki, 0)),
        pl.BlockSpec((None, tk, V_EXT_W), lambda bi, qi, ki: (bi, ki, 0)),
    ]
    out_specs = [pl.BlockSpec((None, tq, ATTN_W), lambda bi, qi, ki: (bi, qi, 0)),
                 pl.BlockSpec((None, None, 8, LANES), lambda bi, qi, ki: (bi, qi, 0, 0))]
    out_shape = [jax.ShapeDtypeStruct((b, t, ATTN_W), BF16),
                 jax.ShapeDtypeStruct((b, nq, 8, LANES), F32)]
    scratch = [pltpu.VMEM((ATTN_HEADS, tq, LANES), F32),
               pltpu.VMEM((ATTN_HEADS, tq, V_EXT_W), F32)]
    args = (q, kc, vc, k, v)
    if not online:
        c = dqkv.shape[2]
        ct = t // (nq * nk)
        assert ct % HALO == 0 and ct * nq * nk == t
        per = ct // HALO
        last = t // HALO - 1
        tile = lambda qi, ki: qi * nk + ki
        in_specs += [
            pl.BlockSpec((None, ct, c), lambda bi, qi, ki: (bi, tile(qi, ki), 0)),
            pl.BlockSpec((None, HALO, c), lambda bi, qi, ki: (bi, jnp.maximum(tile(qi, ki) * per - 1, 0), 0)),
            pl.BlockSpec((None, HALO, c), lambda bi, qi, ki: (bi, jnp.minimum((tile(qi, ki) + 1) * per, last), 0)),
            pl.BlockSpec(conv_w.shape, lambda bi, qi, ki: (0, 0)),
        ]
        out_specs.append(pl.BlockSpec((None, ct, c), lambda bi, qi, ki: (bi, tile(qi, ki), 0)))
        out_shape.append(jax.ShapeDtypeStruct((b, t, c), F32))
        scratch.append(pltpu.VMEM((ct + 2 * HALO, c), F32))
        args += (dqkv, dqkv, dqkv, conv_w)
    return pl.pallas_call(
        functools.partial(_attn_kernel, online),
        grid=(b, nq, nk),
        in_specs=in_specs,
        out_specs=out_specs,
        out_shape=out_shape,
        scratch_shapes=scratch,
        compiler_params=_cparams(("parallel", "parallel", "arbitrary")),
        name="attention_online" if online else "attention",
    )(*args)


def _dot_nt(a, b):
    return lax.dot_general(a, b, (((1,), (1,)), ((), ())), preferred_element_type=F32)


def _dot_tn(a, b):
    return lax.dot_general(a, b, (((0,), (0,)), ((), ())), preferred_element_type=F32)


def _unit_tri_inverses(ns, row, col):
    eye = (row == col).astype(F32)
    ds = None
    s = 1
    while s < CHUNK:
        sh = s.bit_length() - 1
        join = ((row >> (sh + 1)) == (col >> (sh + 1))) & ((row >> sh) != (col >> sh))
        es = [jnp.where(join, n, 0.0) for n in ns]
        if s == 1:
            ds = [eye - e for e in es]
        else:
            eds = [_dot_bf16(e, d) for e, d in zip(es, ds)]
            ds = [d - _dot_bf16(d, ed) for d, ed in zip(ds, eds)]
        s *= 2
    return ds


def _gdn_chunk(has_q, nb, sub, n_sub, xf_ref, xb_ref, gf_ref, gb_ref, tf_ref, tb_ref, of_ref, ob_ref, s_ref):
    row = lax.broadcasted_iota(jnp.int32, (CHUNK, CHUNK), 0)
    col = lax.broadcasted_iota(jnp.int32, (CHUNK, CHUNK), 1)
    off_k = DN_W if has_q else 0
    off_v = off_k + DN_W
    scale = DN_HEAD_DIM ** -0.5
    units = [(bi, d, h) for bi in range(nb) for d in range(N_DIR) for h in range(DN_HEADS)]

    ops = []
    for bi, d, h in units:
        x_ref, g_ref, t_ref = (xf_ref, gf_ref, tf_ref) if d == 0 else (xb_ref, gb_ref, tb_ref)
        c = sub if d == 0 else n_sub - 1 - sub
        rows = slice(c * CHUNK, (c + 1) * CHUNK)
        incl = (row >= col) if d == 0 else (row <= col)
        strict = (row > col) if d == 0 else (row < col)
        u = d * DN_HEADS + h

        def gate_col(kind):
            return g_ref[bi, rows, kind * N_GATE + u:kind * N_GATE + u + 1]

        beta, gc, e_gc, e_dec = gate_col(0), gate_col(1), gate_col(2), gate_col(3)
        e_tot = g_ref[bi, c * CHUNK:c * CHUNK + 1, 4 * N_GATE + u:4 * N_GATE + u + 1]
        gc_t = t_ref[bi, c, u:u + 1, :]
        head = lambda off: x_ref[bi, rows, off + h * DN_HEAD_DIM:off + (h + 1) * DN_HEAD_DIM]
        k, v = head(off_k), head(off_v)
        k16 = k.astype(BF16)
        kb = k * beta
        decay = jnp.exp(jnp.where(incl, gc - gc_t, -jnp.inf))
        if has_q:
            q = head(0) * scale
            kq = _dot_nt(jnp.concatenate([kb, q], axis=0).astype(BF16), k16)
            kk, qk = kq[:CHUNK], kq[CHUNK:]
            intra = (qk * decay).astype(BF16)
            qd = (q * e_gc).astype(BF16)
        else:
            kk = _dot_nt(kb.astype(BF16), k16)
            intra = qd = None
        ops.append(dict(
            n=jnp.where(strict, kk * decay, 0.0),
            rhs=jnp.concatenate([v * beta, kb * e_gc], axis=1).astype(BF16),
            k_dec=(k * e_dec).astype(BF16), e_tot=e_tot, intra=intra, qd=qd, rows=rows))

    t_invs = _unit_tri_inverses([o["n"] for o in ops], row, col)
    uws = [jnp.dot(t.astype(BF16), o["rhs"], preferred_element_type=F32) for t, o in zip(t_invs, ops)]

    reads = []
    for idx, (o, uw) in enumerate(zip(ops, uws)):
        s_old = s_ref[idx]
        w = uw[:, DN_HEAD_DIM:].astype(BF16)
        lhs = jnp.concatenate([w, o["qd"]], axis=0) if has_q else w
        reads.append((s_old, jnp.dot(lhs, s_old.astype(BF16), preferred_element_type=F32)))

    for idx, ((bi, d, h), o, uw, (s_old, rd)) in enumerate(zip(units, ops, uws, reads)):
        v_new = (uw[:, :DN_HEAD_DIM] - rd[:CHUNK]).astype(BF16)
        s_ref[idx] = s_old * o["e_tot"] + _dot_tn(o["k_dec"], v_new)
        if has_q:
            out = rd[CHUNK:] + jnp.dot(o["intra"], v_new, preferred_element_type=F32)
            o_ref = of_ref if d == 0 else ob_ref
            o_ref[bi, o["rows"], h * DN_HEAD_DIM:(h + 1) * DN_HEAD_DIM] = out


def _gdn_kernel(has_q, *refs):
    if has_q:
        (xf_ref, xb_ref, gf_ref, gb_ref, tf_ref, tb_ref, s0_ref, of_ref, ob_ref, sfin_ref, s_ref) = refs
    else:
        (xf_ref, xb_ref, gf_ref, gb_ref, tf_ref, tb_ref, s0_ref, sfin_ref, s_ref) = refs
        of_ref = ob_ref = None
    i = pl.program_id(0)
    n_steps = pl.num_programs(0)
    nb = xf_ref.shape[0]
    n_sub = xf_ref.shape[1] // CHUNK

    @pl.when(i == 0)
    def _():
        s_ref[...] = s0_ref[...]

    for sub in range(n_sub):
        _gdn_chunk(has_q, nb, sub, n_sub, xf_ref, xb_ref, gf_ref, gb_ref, tf_ref, tb_ref, of_ref, ob_ref, s_ref)

    @pl.when(i == n_steps - 1)
    def _():
        sfin_ref[...] = s_ref[...]


def _gdn_scan(x, gates, gates_t, s0, has_q, n_sub):
    b, t, c = x.shape
    rows = n_sub * CHUNK
    n_steps = t // rows
    n_units = b * N_GATE
    fwd = lambda i: (0, i, 0)
    bwd = lambda i: (0, n_steps - 1 - i, 0)
    state_spec = pl.BlockSpec((n_units, DN_HEAD_DIM, DN_HEAD_DIM), lambda i: (0, 0, 0))
    state_shape = jax.ShapeDtypeStruct((n_units, DN_HEAD_DIM, DN_HEAD_DIM), F32)
    out_specs = [state_spec]
    out_shape = [state_shape]
    if has_q:
        o_shape = jax.ShapeDtypeStruct((b, t, DN_W), F32)
        out_specs = [pl.BlockSpec((b, rows, DN_W), fwd), pl.BlockSpec((b, rows, DN_W), bwd)] + out_specs
        out_shape = [o_shape, o_shape] + out_shape
    return pl.pallas_call(
        functools.partial(_gdn_kernel, has_q),
        grid=(n_steps,),
        in_specs=[pl.BlockSpec((b, rows, c), fwd), pl.BlockSpec((b, rows, c), bwd),
                  pl.BlockSpec((b, rows, GATE_W), fwd), pl.BlockSpec((b, rows, GATE_W), bwd),
                  pl.BlockSpec((b, n_sub, N_GATE, CHUNK), lambda i: (0, i, 0, 0)),
                  pl.BlockSpec((b, n_sub, N_GATE, CHUNK), lambda i: (0, n_steps - 1 - i, 0, 0)),
                  state_spec],
        out_specs=out_specs,
        out_shape=out_shape,
        scratch_shapes=[pltpu.VMEM((n_units, DN_HEAD_DIM, DN_HEAD_DIM), F32)],
        compiler_params=_cparams(("arbitrary",)),
        name="gdn_scan" if has_q else "gdn_scan_ctx",
    )(x, x, gates, gates, gates_t, gates_t, s0)


def _out_mlp_kernel(ff_tile, x_ref, oa_ref, of_ref, ob_ref, sz_ref, mod_ref, dnw_ref, wo_ref, n2w_ref,
                    w1_ref, w2_ref, o_ref):
    o_dn = of_ref[...] + ob_ref[...]
    parts = [oa_ref[...]]
    for h in range(DN_HEADS):
        cols = slice(h * DN_HEAD_DIM, (h + 1) * DN_HEAD_DIM)
        seg = o_dn[:, cols]
        seg = seg * lax.rsqrt(jnp.mean(seg * seg, axis=-1, keepdims=True) + NORM_EPS) * dnw_ref[...]
        parts.append((seg * sz_ref[:, cols]).astype(BF16))
    mixed = jnp.dot(jnp.concatenate(parts, axis=1), wo_ref[...], preferred_element_type=F32)
    x = x_ref[...] + mod_ref[2:3, :] * mixed

    y = x * lax.rsqrt(jnp.mean(x * x, axis=-1, keepdims=True) + NORM_EPS) * n2w_ref[...]
    h2 = (y * (1.0 + mod_ref[4:5, :]) + mod_ref[3:4, :]).astype(BF16)
    acc = None
    for c in range(w1_ref.shape[1] // ff_tile):
        a = jnp.dot(h2, w1_ref[:, c * ff_tile:(c + 1) * ff_tile], preferred_element_type=F32)
        a = jnp.square(jnp.maximum(a, 0.0)).astype(BF16)
        part = jnp.dot(a, w2_ref[c * ff_tile:(c + 1) * ff_tile, :], preferred_element_type=F32)
        acc = part if acc is None else acc + part
    o_ref[...] = x + mod_ref[5:6, :] * acc


def _out_mlp(x, o_attn, o_f, o_b, sz, mod, dnw, w_out, n2w, w1, w2, tm, ff_tile):
    b, t, d = x.shape
    row = lambda bi, i: (bi, i, 0)
    full = lambda bi, i: (0, 0)
    resident = lambda w: pl.BlockSpec(w.shape, full, pipeline_mode=pl.Buffered(1))
    return pl.pallas_call(
        functools.partial(_out_mlp_kernel, ff_tile),
        grid=(b, t // tm),
        in_specs=[pl.BlockSpec((None, tm, d), row),
                  pl.BlockSpec((None, tm, ATTN_W), row),
                  pl.BlockSpec((None, tm, DN_W), row),
                  pl.BlockSpec((None, tm, DN_W), row),
                  pl.BlockSpec((None, tm, DN_W), row),
                  pl.BlockSpec((None, N_MOD, d), lambda bi, i: (bi, 0, 0)),
                  pl.BlockSpec((1, DN_HEAD_DIM), full),
                  resident(w_out),
                  pl.BlockSpec((1, d), full),
                  resident(w1),
                  resident(w2)],
        out_specs=pl.BlockSpec((None, tm, d), row),
        out_shape=jax.ShapeDtypeStruct((b, t, d), F32),
        compiler_params=_cparams(("parallel", "parallel")),
        name="out_mlp",
    )(x, o_attn, o_f, o_b, sz, mod, dnw, w_out, n2w, w1, w2)


def _rope_tables(t):
    half = ATTN_HEAD_DIM // 4
    inv = ROPE_THETA ** (-jnp.arange(half, dtype=F32) / half)
    pos = jnp.arange(t, dtype=jnp.int32)
    ang_r = (pos // GRID_W).astype(F32)[:, None] * inv[None, :]
    ang_c = (pos % GRID_W).astype(F32)[:, None] * inv[None, :]
    cos = jnp.concatenate([jnp.cos(ang_r)] * 2 + [jnp.cos(ang_c)] * 2, axis=1)
    sin = jnp.concatenate([-jnp.sin(ang_r), jnp.sin(ang_r), -jnp.sin(ang_c), jnp.sin(ang_c)], axis=1)
    rep = LANES // ATTN_HEAD_DIM
    return jnp.tile(cos, (1, rep)), jnp.tile(sin, (1, rep))


def _pad_lanes(vec, offset):
    return jnp.zeros((1, GATE_W), F32).at[0, offset:offset + vec.size].set(vec.reshape(-1).astype(F32))


def kernel(x, c, ctx, c_ctx, w_mod, b_mod, norm1_w, w_in, q_norm_w, k_norm_w, conv_w, a_log, dt_bias,
           dn_norm_w, w_out, norm2_w, w_mlp1, w_mlp2):
    b, t, d = x.shape
    n_ctx = ctx.shape[1]
    depth = w_mod.shape[0]
    cos, sin = _rope_tables(t)
    head_id = jnp.arange(ATTN_W, dtype=jnp.int32) // ATTN_HEAD_DIM
    gmat = (head_id[:, None] == head_id[None, :]).astype(BF16)
    c_rows = jnp.zeros((8, d), F32).at[:b].set(c).at[b].set(c_ctx)
    tm = min(512, t)
    for l in range(depth):
        mod = _modulation(c_rows, w_mod, b_mod[l], l).reshape(8, N_MOD, d)
        w_in_p = jnp.pad(w_in[l], ((0, 0), (0, IN_COLS_PAD - w_in.shape[2]))).astype(BF16)
        consts = (norm1_w[l].reshape(1, d), w_in_p,
                  jnp.tile(q_norm_w[l], ATTN_HEADS).reshape(1, ATTN_W),
                  jnp.tile(k_norm_w[l], ATTN_KV_HEADS).reshape(1, ATTN_KV_W),
                  cos, sin, gmat, _pad_lanes(a_log[l], N_GATE), _pad_lanes(dt_bias[l], N_GATE))
        q_a, k_a, v_a, dqkv, sz, gates, gates_t = _inproj(
            x, mod, lambda bi, i: (bi, 0, 0), False, tm, consts + (conv_w[l],))
        k_c, v_c, ckv, cgates, cgates_t = _inproj(
            ctx, mod, lambda bi, i: (b, 0, 0), True, n_ctx, consts + (conv_w[l][:, DN_W:],))

        attn_args = (q_a, k_a, v_a, k_c, v_c, min(1024, t), min(2048, t))
        o_fixed, bad, qkv = _attention(False, *attn_args, dqkv=dqkv, conv_w=conv_w[l])
        o_attn = lax.cond(jnp.max(bad) > 0.0, lambda: _attention(True, *attn_args)[0], lambda: o_fixed)

        s_zero = jnp.zeros((b * N_GATE, DN_HEAD_DIM, DN_HEAD_DIM), F32)
        (s_ctx,) = _gdn_scan(ckv, cgates, cgates_t, s_zero, False, 4)
        o_f, o_b, _ = _gdn_scan(qkv, gates, gates_t, s_ctx, True, 4)

        x = _out_mlp(x, o_attn, o_f, o_b, sz, mod, dn_norm_w[l].reshape(1, DN_HEAD_DIM),
                     w_out[l].astype(BF16), norm2_w[l].reshape(1, d), w_mlp1[l].astype(BF16),
                     w_mlp2[l].astype(BF16), tm, 1024)
    return x
```

```python
import functools

import jax
import jax.numpy as jnp
from jax import lax
from jax.experimental import pallas as pl
from jax.experimental.pallas import tpu as pltpu

F32 = jnp.float32
BF16 = jnp.bfloat16

GRID_W = 64
ATTN_HEAD_DIM = 64
ATTN_HEADS = 8
ATTN_KV_HEADS = 2
ATTN_GROUP = ATTN_HEADS // ATTN_KV_HEADS
ATTN_W = ATTN_HEADS * ATTN_HEAD_DIM
ATTN_KV_W = ATTN_KV_HEADS * ATTN_HEAD_DIM
DN_HEAD_DIM = 128
DN_HEADS = 4
DN_W = DN_HEADS * DN_HEAD_DIM
N_DIR = 2
CONV_K = 5
CHUNK = 64
N_MOD = 6
ROPE_THETA = 10000.0
NORM_EPS = 1e-6
LANES = 128
GATE_W = LANES
N_GATE = N_DIR * DN_HEADS

OFF_Q = 0
OFF_K = OFF_Q + ATTN_W
OFF_V = OFF_K + ATTN_KV_W
OFF_DQKV = OFF_V + ATTN_KV_W
OFF_Z = OFF_DQKV + 3 * DN_W
OFF_GATE = OFF_Z + DN_W
IN_COLS_PAD = OFF_GATE + GATE_W

HALO = 8
Q_EXT_W = ATTN_HEADS * LANES
V_EXT_W = 2 * LANES

VMEM_LIMIT = 56 * 1024 * 1024


def _cparams(sem):
    return pltpu.CompilerParams(dimension_semantics=sem, vmem_limit_bytes=VMEM_LIMIT)


def _dot_bf16(a, b):
    return jnp.dot(a.astype(BF16), b.astype(BF16), preferred_element_type=F32)


def _dot_f32(a, b):
    return jnp.dot(a, b, preferred_element_type=F32, precision=lax.Precision.HIGHEST)


def _split_bf16(a):
    hi = a.astype(BF16)
    lo = (a - hi.astype(F32)).astype(BF16)
    return hi, lo


def _silu(x):
    return x * jax.nn.sigmoid(x)


def _mod_kernel(c_ref, w_ref, b_ref, o_ref):
    o_ref[...] = _dot_f32(_silu(c_ref[...]), w_ref[...]) + b_ref[...]


def _modulation(c_rows, w_mod, b_mod, layer):
    r, d = c_rows.shape
    n = w_mod.shape[2]
    tn = 1024
    return pl.pallas_call(
        _mod_kernel,
        grid=(n // tn,),
        in_specs=[pl.BlockSpec((r, d), lambda j: (0, 0)),
                  pl.BlockSpec((None, d, tn), lambda j: (layer, 0, j)),
                  pl.BlockSpec((1, tn), lambda j: (0, j))],
        out_specs=pl.BlockSpec((r, tn), lambda j: (0, j)),
        out_shape=jax.ShapeDtypeStruct((r, n), F32),
        compiler_params=_cparams(("arbitrary",)),
        name="modulation",
    )(c_rows, w_mod, b_mod.reshape(1, n))


def _group_mean_sq(a, gmat, group):
    hi, lo = _split_bf16(a * a)
    return jnp.dot(jnp.concatenate([hi, lo], axis=1), jnp.concatenate([gmat, gmat], axis=0),
                   preferred_element_type=F32) * (1.0 / group)


def _rope(a, cos, sin):
    parts = []
    for c in range(a.shape[1] // LANES):
        ac = a[:, c * LANES:(c + 1) * LANES]
        lane = lax.broadcasted_iota(jnp.int32, ac.shape, 1)
        nxt = pltpu.roll(ac, LANES - 16, 1)
        prv = pltpu.roll(ac, 16, 1)
        parts.append(jnp.where((lane & 31) < 16, nxt, prv))
    partner = parts[0] if len(parts) == 1 else jnp.concatenate(parts, axis=1)
    return a * cos + partner * sin


def _chunk_scans(x):
    n = x.shape[0]
    pos = lax.broadcasted_iota(jnp.int32, x.shape, 0) & (CHUNK - 1)
    fwd, rev = x, x
    s = 1
    while s < CHUNK:
        fwd = fwd + jnp.where(pos >= s, pltpu.roll(fwd, s, 0), 0.0)
        rev = rev + jnp.where(pos + s < CHUNK, pltpu.roll(rev, n - s, 0), 0.0)
        s *= 2
    return fwd, rev


def _gate_maps(raw, alog, dtb):
    lane = lax.broadcasted_iota(jnp.int32, raw.shape, 1)
    beta = jax.nn.sigmoid(raw)
    y = raw + dtb
    softplus = jnp.maximum(y, 0.0) + jnp.log(1.0 + jnp.exp(-jnp.abs(y)))
    g = -jnp.exp(alog) * softplus
    fwd, rev = _chunk_scans(g)
    gc = jnp.where(lane < N_GATE + DN_HEADS, fwd, rev)
    total = fwd + rev - g
    e_gc = pltpu.roll(jnp.exp(gc), N_GATE, 1)
    e_dec = pltpu.roll(jnp.exp(total - gc), 2 * N_GATE, 1)
    e_tot = pltpu.roll(jnp.exp(total), 3 * N_GATE, 1)
    out = jnp.where(lane < N_GATE, beta,
                    jnp.where(lane < 2 * N_GATE, gc,
                              jnp.where(lane < 3 * N_GATE, e_gc,
                                        jnp.where(lane < 4 * N_GATE, e_dec, e_tot))))
    return out, gc


def _conv_silu_norm(win_ref, w_ref, o_ref, n_norm, groups=None):
    tm = o_ref.shape[0]
    for hd in (range(o_ref.shape[1] // LANES) if groups is None else groups):
        cols = slice(hd * LANES, (hd + 1) * LANES)
        win = win_ref[:, cols]
        acc = None
        for j in range(CONV_K):
            shift = (CONV_K // 2 - j) % win.shape[0]
            tap = win if shift == 0 else pltpu.roll(win, shift, 0)
            term = tap[HALO:HALO + tm] * w_ref[j:j + 1, cols]
            acc = term if acc is None else acc + term
        y = _silu(acc)
        if hd < n_norm:
            y = y * lax.rsqrt(jnp.sum(y * y, axis=-1, keepdims=True) + NORM_EPS)
        o_ref[:, cols] = y


def _inproj_kernel(is_ctx, x_ref, mod_ref, n1w_ref, w_ref, qnw_ref, knw_ref, cos_ref, sin_ref,
                   gmat_ref, alog_ref, dtb_ref, cw_ref, *refs):
    x = x_ref[...]
    tm = x.shape[0]
    y = x * lax.rsqrt(jnp.mean(x * x, axis=-1, keepdims=True) + NORM_EPS) * n1w_ref[...]
    h = (y * (1.0 + mod_ref[1:2, :]) + mod_ref[0:1, :]).astype(BF16)

    def proj(lo, hi):
        return jnp.dot(h, w_ref[:, lo:hi], preferred_element_type=F32)

    kv = proj(OFF_K, OFF_DQKV)
    kk = kv[:, :ATTN_KV_W]
    kk = kk * lax.rsqrt(_group_mean_sq(kk, gmat_ref[:ATTN_KV_W, :ATTN_KV_W], ATTN_HEAD_DIM)
                        + NORM_EPS) * knw_ref[...]
    vv = kv[:, ATTN_KV_W:]
    gates, gc = _gate_maps(proj(OFF_GATE, IN_COLS_PAD), alog_ref[...], dtb_ref[...])
    if is_ctx:
        k_out, v_out, conv_out, gate_out, gt_out, win_ref = refs
        edge = jnp.zeros((HALO, conv_out.shape[1]), F32)
        win_ref[0:HALO, :] = edge
        win_ref[HALO:HALO + tm, :] = proj(OFF_DQKV + DN_W, OFF_Z)
        win_ref[HALO + tm:, :] = edge
        _conv_silu_norm(win_ref, cw_ref, conv_out, DN_HEADS)
    else:
        q_out, k_out, v_out, dqkv_out, sz_out, gate_out, gt_out = refs
        dqkv_out[...] = proj(OFF_DQKV, OFF_Z)
        cos = cos_ref[...]
        sin = sin_ref[...]
        kk = _rope(kk, cos, sin)
        qq = proj(OFF_Q, OFF_K)
        qq = qq * lax.rsqrt(_group_mean_sq(qq, gmat_ref[...], ATTN_HEAD_DIM) + NORM_EPS) * qnw_ref[...]
        rep = ATTN_W // LANES
        qq = _rope(qq, jnp.concatenate([cos] * rep, axis=1), jnp.concatenate([sin] * rep, axis=1))
        qb = (qq * (ATTN_HEAD_DIM ** -0.5)).astype(q_out.dtype)
        zeros = jnp.zeros((qb.shape[0], ATTN_HEAD_DIM), q_out.dtype)
        for hd in range(ATTN_HEADS):
            seg = qb[:, hd * ATTN_HEAD_DIM:(hd + 1) * ATTN_HEAD_DIM]
            pair = [seg, zeros] if hd // ATTN_GROUP == 0 else [zeros, seg]
            q_out[:, hd * LANES:(hd + 1) * LANES] = jnp.concatenate(pair, axis=1)
        sz_out[...] = _silu(proj(OFF_Z, OFF_GATE))
    k_out[...] = kk.astype(k_out.dtype)
    v_out[:, :ATTN_KV_W] = vv.astype(v_out.dtype)
    v_out[:, ATTN_KV_W:] = jnp.ones((vv.shape[0], V_EXT_W - ATTN_KV_W), v_out.dtype)
    gate_out[...] = gates
    gc_t = gc.T
    for ch in range(gt_out.shape[0]):
        gt_out[ch] = gc_t[N_GATE:2 * N_GATE, ch * CHUNK:(ch + 1) * CHUNK]


def _inproj(x, mod, mod_index, is_ctx, tm, consts):
    b, t, d = x.shape
    n1w, w_in, qnw, knw, cos, sin, gmat, alog, dtb, conv_w = consts
    nt = t // tm
    assert not is_ctx or nt == 1, "the context call convolves in-kernel and needs the whole sequence in one tile"
    row = lambda bi, i: (bi, i, 0)
    full = lambda bi, i: (0, 0)
    conv_cols = conv_w.shape[1]
    in_specs = [
        pl.BlockSpec((None, tm, d), row),
        pl.BlockSpec((None, N_MOD, d), mod_index),
        pl.BlockSpec((1, d), full),
        pl.BlockSpec(w_in.shape, full),
        pl.BlockSpec(qnw.shape, full),
        pl.BlockSpec(knw.shape, full),
        pl.BlockSpec((tm, LANES), lambda bi, i: (i, 0)),
        pl.BlockSpec((tm, LANES), lambda bi, i: (i, 0)),
        pl.BlockSpec(gmat.shape, full),
        pl.BlockSpec((1, GATE_W), full),
        pl.BlockSpec((1, GATE_W), full),
        pl.BlockSpec(conv_w.shape, full),
    ]

    def out(width, dtype):
        return pl.BlockSpec((None, tm, width), row), jax.ShapeDtypeStruct((b, t, width), dtype)

    if is_ctx:
        outs = [out(ATTN_KV_W, BF16), out(V_EXT_W, BF16), out(conv_cols, F32), out(GATE_W, F32)]
    else:
        outs = [out(Q_EXT_W, BF16), out(ATTN_KV_W, BF16), out(V_EXT_W, BF16),
                out(conv_cols, F32), out(DN_W, F32), out(GATE_W, F32)]
    outs.append((pl.BlockSpec((None, tm // CHUNK, N_GATE, CHUNK), lambda bi, i: (bi, i, 0, 0)),
                 jax.ShapeDtypeStruct((b, t // CHUNK, N_GATE, CHUNK), F32)))
    return pl.pallas_call(
        functools.partial(_inproj_kernel, is_ctx),
        grid=(b, nt),
        in_specs=in_specs,
        out_specs=[o[0] for o in outs],
        out_shape=[o[1] for o in outs],
        scratch_shapes=[pltpu.VMEM((tm + 2 * HALO, conv_cols), F32)] if is_ctx else [],
        compiler_params=_cparams(("parallel", "parallel")),
        name="inproj_ctx" if is_ctx else "inproj",
    )(x, mod, n1w, w_in, qnw, knw, cos, sin, gmat, alog, dtb, conv_w)


def _attn_kernel(online, q_ref, kc_ref, vc_ref, k_ref, v_ref, *refs):
    if online:
        o_ref, bad_ref, m_ref, acc_ref = refs
    else:
        d_ref, dp_ref, dn_ref, cw_ref, o_ref, bad_ref, conv_ref, m_ref, acc_ref, win_ref = refs
    ki = pl.program_id(2)
    nk = pl.num_programs(2)

    def block(kb, vb, first, side_job=None):
        reps = kb.shape[0] // LANES

        def scores(h):
            return lax.dot_general(q_ref[:, h * LANES:(h + 1) * LANES], kb, (((1,), (1,)), ((), ())),
                                   preferred_element_type=F32)

        s_next = scores(0)
        for h in range(ATTN_HEADS):
            s = s_next
            if h + 1 < ATTN_HEADS:
                s_next = scores(h + 1)
            if first:
                m = jnp.broadcast_to(jnp.max(s, axis=-1, keepdims=True), m_ref.shape[1:])
                m_ref[h] = m
            elif online:
                m_prev = m_ref[h]
                m = jnp.maximum(m_prev, jnp.max(s, axis=-1, keepdims=True))
                alpha = jnp.exp(m_prev - m)
                m_ref[h] = m
            else:
                m = m_ref[h]
            p = jnp.exp(s - jnp.concatenate([m] * reps, axis=1))
            pv = jnp.dot(p.astype(BF16), vb, preferred_element_type=F32)
            if first:
                acc_ref[h] = pv
            elif online:
                acc_ref[h] = jnp.concatenate([alpha] * (V_EXT_W // LANES), axis=1) * acc_ref[h] + pv
            else:
                acc_ref[h] += pv
            if side_job is not None:
                side_job(h)

    @pl.when(ki == 0)
    def _():
        block(kc_ref[...], vc_ref[...], True)

    if online:
        block(k_ref[...], v_ref[...], False)
    else:
        tile = pl.program_id(1) * nk + ki
        n_tiles = pl.num_programs(1) * nk
        ct = d_ref.shape[0]
        win_ref[0:HALO, :] = jnp.where(tile > 0, dp_ref[...], 0.0)
        win_ref[HALO:HALO + ct, :] = d_ref[...]
        win_ref[HALO + ct:, :] = jnp.where(tile < n_tiles - 1, dn_ref[...], 0.0)
        n_groups = conv_ref.shape[1] // LANES

        def conv_share(h):
            groups = range(h * n_groups // ATTN_HEADS, (h + 1) * n_groups // ATTN_HEADS)
            _conv_silu_norm(win_ref, cw_ref, conv_ref, 2 * DN_HEADS, groups)

        block(k_ref[...], v_ref[...], False, conv_share)

    @pl.when(ki == nk - 1)
    def _():
        bad = jnp.zeros((1, 1), F32)
        for h in range(ATTN_HEADS):
            j = h // ATTN_GROUP
            acc = acc_ref[h]
            num = acc[:, j * ATTN_HEAD_DIM:(j + 1) * ATTN_HEAD_DIM]
            den = acc[:, ATTN_KV_W + j * ATTN_HEAD_DIM:ATTN_KV_W + (j + 1) * ATTN_HEAD_DIM]
            out = num / den
            o_ref[:, h * ATTN_HEAD_DIM:(h + 1) * ATTN_HEAD_DIM] = out.astype(o_ref.dtype)
            bad = jnp.maximum(bad, jnp.max(jnp.where(jnp.isfinite(out), 0.0, 1.0), keepdims=True))
        bad_ref[...] = jnp.broadcast_to(bad, bad_ref.shape)


def _attention(online, q, k, v, kc, vc, tq, tk, dqkv=None, conv_w=None):
    b, t, _ = q.shape
    n_ctx = kc.shape[1]
    nq, nk = t // tq, t // tk
    in_specs = [
        pl.BlockSpec((None, tq, Q_EXT_W), lambda bi, qi, ki: (bi, qi, 0)),
        pl.BlockSpec((None, n_ctx, ATTN_KV_W), lambda bi, qi, ki: (bi, 0, 0)),
        pl.BlockSpec((None, n_ctx, V_EXT_W), lambda bi, qi, ki: (bi, 0, 0)),
        pl.BlockSpec((None, tk, ATTN_KV_W), lambda bi, qi, ki: (bi, ki, 0)),
        pl.BlockSpec((None, tk, V_EXT_W), lambda bi, qi, ki: (bi, ki, 0)),
    ]
    out_specs = [pl.BlockSpec((None, tq, ATTN_W), lambda bi, qi, ki: (bi, qi, 0)),
                 pl.BlockSpec((None, None, 8, LANES), lambda bi, qi, ki: (bi, qi, 0, 0))]
    out_shape = [jax.ShapeDtypeStruct((b, t, ATTN_W), BF16),
                 jax.ShapeDtypeStruct((b, nq, 8, LANES), F32)]
    scratch = [pltpu.VMEM((ATTN_HEADS, tq, LANES), F32),
               pltpu.VMEM((ATTN_HEADS, tq, V_EXT_W), F32)]
    args = (q, kc, vc, k, v)
    if not online:
        c = dqkv.shape[2]
        ct = t // (nq * nk)
        assert ct % HALO == 0 and ct * nq * nk == t
        per = ct // HALO
        last = t // HALO - 1
        tile = lambda qi, ki: qi * nk + ki
        in_specs += [
            pl.BlockSpec((None, ct, c), lambda bi, qi, ki: (bi, tile(qi, ki), 0)),
            pl.BlockSpec((None, HALO, c), lambda bi, qi, ki: (bi, jnp.maximum(tile(qi, ki) * per - 1, 0), 0)),
            pl.BlockSpec((None, HALO, c), lambda bi, qi, ki: (bi, jnp.minimum((tile(qi, ki) + 1) * per, last), 0)),
            pl.BlockSpec(conv_w.shape, lambda bi, qi, ki: (0, 0)),
        ]
        out_specs.append(pl.BlockSpec((None, ct, c), lambda bi, qi, ki: (bi, tile(qi, ki), 0)))
        out_shape.append(jax.ShapeDtypeStruct((b, t, c), F32))
        scratch.append(pltpu.VMEM((ct + 2 * HALO, c), F32))
        args += (dqkv, dqkv, dqkv, conv_w)
    return pl.pallas_call(
        functools.partial(_attn_kernel, online),
        grid=(b, nq, nk),
        in_specs=in_specs,
        out_specs=out_specs,
        out_shape=out_shape,
        scratch_shapes=scratch,
        compiler_params=_cparams(("parallel", "parallel", "arbitrary")),
        name="attention_online" if online else "attention",
    )(*args)


def _dot_nt(a, b):
    return lax.dot_general(a, b, (((1,), (1,)), ((), ())), preferred_element_type=F32)


def _dot_tn(a, b):
    return lax.dot_general(a, b, (((0,), (0,)), ((), ())), preferred_element_type=F32)


def _unit_tri_inverses(ns, row, col):
    eye = (row == col).astype(F32)
    ds = None
    s = 1
    while s < CHUNK:
        sh = s.bit_length() - 1
        join = ((row >> (sh + 1)) == (col >> (sh + 1))) & ((row >> sh) != (col >> sh))
        es = [jnp.where(join, n, 0.0) for n in ns]
        if s == 1:
            ds = [eye - e for e in es]
        else:
            eds = [_dot_bf16(e, d) for e, d in zip(es, ds)]
            ds = [d - _dot_bf16(d, ed) for d, ed in zip(ds, eds)]
        s *= 2
    return ds


def _gdn_chunk(has_q, nb, sub, n_sub, xf_ref, xb_ref, gf_ref, gb_ref, tf_ref, tb_ref, of_ref, ob_ref, s_ref):
    row = lax.broadcasted_iota(jnp.int32, (CHUNK, CHUNK), 0)
    col = lax.broadcasted_iota(jnp.int32, (CHUNK, CHUNK), 1)
    off_k = DN_W if has_q else 0
    off_v = off_k + DN_W
    scale = DN_HEAD_DIM ** -0.5
    units = [(bi, d, h) for bi in range(nb) for d in range(N_DIR) for h in range(DN_HEADS)]

    ops = []
    for bi, d, h in units:
        x_ref, g_ref, t_ref = (xf_ref, gf_ref, tf_ref) if d == 0 else (xb_ref, gb_ref, tb_ref)
        c = sub if d == 0 else n_sub - 1 - sub
        rows = slice(c * CHUNK, (c + 1) * CHUNK)
        incl = (row >= col) if d == 0 else (row <= col)
        strict = (row > col) if d == 0 else (row < col)
        u = d * DN_HEADS + h

        def gate_col(kind):
            return g_ref[bi, rows, kind * N_GATE + u:kind * N_GATE + u + 1]

        beta, gc, e_gc, e_dec = gate_col(0), gate_col(1), gate_col(2), gate_col(3)
        e_tot = g_ref[bi, c * CHUNK:c * CHUNK + 1, 4 * N_GATE + u:4 * N_GATE + u + 1]
        gc_t = t_ref[bi, c, u:u + 1, :]
        head = lambda off: x_ref[bi, rows, off + h * DN_HEAD_DIM:off + (h + 1) * DN_HEAD_DIM]
        k, v = head(off_k), head(off_v)
        k16 = k.astype(BF16)
        kb = k * beta
        decay = jnp.exp(jnp.where(incl, gc - gc_t, -jnp.inf))
        if has_q:
            q = head(0) * scale
            kq = _dot_nt(jnp.concatenate([kb, q], axis=0).astype(BF16), k16)
            kk, qk = kq[:CHUNK], kq[CHUNK:]
            intra = (qk * decay).astype(BF16)
            qd = (q * e_gc).astype(BF16)
        else:
            kk = _dot_nt(kb.astype(BF16), k16)
            intra = qd = None
        ops.append(dict(
            n=jnp.where(strict, kk * decay, 0.0),
            rhs=jnp.concatenate([v * beta, kb * e_gc], axis=1).astype(BF16),
            k_dec=(k * e_dec).astype(BF16), e_tot=e_tot, intra=intra, qd=qd, rows=rows))

    t_invs = _unit_tri_inverses([o["n"] for o in ops], row, col)
    uws = [jnp.dot(t.astype(BF16), o["rhs"], preferred_element_type=F32) for t, o in zip(t_invs, ops)]

    reads = []
    for idx, (o, uw) in enumerate(zip(ops, uws)):
        s_old = s_ref[idx]
        w = uw[:, DN_HEAD_DIM:].astype(BF16)
        lhs = jnp.concatenate([w, o["qd"]], axis=0) if has_q else w
        reads.append((s_old, jnp.dot(lhs, s_old.astype(BF16), preferred_element_type=F32)))

    for idx, ((bi, d, h), o, uw, (s_old, rd)) in enumerate(zip(units, ops, uws, reads)):
        v_new = (uw[:, :DN_HEAD_DIM] - rd[:CHUNK]).astype(BF16)
        s_ref[idx] = s_old * o["e_tot"] + _dot_tn(o["k_dec"], v_new)
        if has_q:
            out = rd[CHUNK:] + jnp.dot(o["intra"], v_new, preferred_element_type=F32)
            o_ref = of_ref if d == 0 else ob_ref
            o_ref[bi, o["rows"], h * DN_HEAD_DIM:(h + 1) * DN_HEAD_DIM] = out


def _gdn_kernel(has_q, *refs):
    if has_q:
        (xf_ref, xb_ref, gf_ref, gb_ref, tf_ref, tb_ref, s0_ref, of_ref, ob_ref, sfin_ref, s_ref) = refs
    else:
        (xf_ref, xb_ref, gf_ref, gb_ref, tf_ref, tb_ref, s0_ref, sfin_ref, s_ref) = refs
        of_ref = ob_ref = None
    i = pl.program_id(0)
    n_steps = pl.num_programs(0)
    nb = xf_ref.shape[0]
    n_sub = xf_ref.shape[1] // CHUNK

    @pl.when(i == 0)
    def _():
        s_ref[...] = s0_ref[...]

    for sub in range(n_sub):
        _gdn_chunk(has_q, nb, sub, n_sub, xf_ref, xb_ref, gf_ref, gb_ref, tf_ref, tb_ref, of_ref, ob_ref, s_ref)

    @pl.when(i == n_steps - 1)
    def _():
        sfin_ref[...] = s_ref[...]


def _gdn_scan(x, gates, gates_t, s0, has_q, n_sub):
    b, t, c = x.shape
    rows = n_sub * CHUNK
    n_steps = t // rows
    n_units = b * N_GATE
    fwd = lambda i: (0, i, 0)
    bwd = lambda i: (0, n_steps - 1 - i, 0)
    state_spec = pl.BlockSpec((n_units, DN_HEAD_DIM, DN_HEAD_DIM), lambda i: (0, 0, 0))
    state_shape = jax.ShapeDtypeStruct((n_units, DN_HEAD_DIM, DN_HEAD_DIM), F32)
    out_specs = [state_spec]
    out_shape = [state_shape]
    if has_q:
        o_shape = jax.ShapeDtypeStruct((b, t, DN_W), F32)
        out_specs = [pl.BlockSpec((b, rows, DN_W), fwd), pl.BlockSpec((b, rows, DN_W), bwd)] + out_specs
        out_shape = [o_shape, o_shape] + out_shape
    return pl.pallas_call(
        functools.partial(_gdn_kernel, has_q),
        grid=(n_steps,),
        in_specs=[pl.BlockSpec((b, rows, c), fwd), pl.BlockSpec((b, rows, c), bwd),
                  pl.BlockSpec((b, rows, GATE_W), fwd), pl.BlockSpec((b, rows, GATE_W), bwd),
                  pl.BlockSpec((b, n_sub, N_GATE, CHUNK), lambda i: (0, i, 0, 0)),
                  pl.BlockSpec((b, n_sub, N_GATE, CHUNK), lambda i: (0, n_steps - 1 - i, 0, 0)),
                  state_spec],
        out_specs=out_specs,
        out_shape=out_shape,
        scratch_shapes=[pltpu.VMEM((n_units, DN_HEAD_DIM, DN_HEAD_DIM), F32)],
        compiler_params=_cparams(("arbitrary",)),
        name="gdn_scan" if has_q else "gdn_scan_ctx",
    )(x, x, gates, gates, gates_t, gates_t, s0)


def _out_mlp_kernel(ff_tile, x_ref, oa_ref, of_ref, ob_ref, sz_ref, mod_ref, dnw_ref, wo_ref, n2w_ref,
                    w1_ref, w2_ref, o_ref):
    o_dn = of_ref[...] + ob_ref[...]
    parts = [oa_ref[...]]
    for h in range(DN_HEADS):
        cols = slice(h * DN_HEAD_DIM, (h + 1) * DN_HEAD_DIM)
        seg = o_dn[:, cols]
        seg = seg * lax.rsqrt(jnp.mean(seg * seg, axis=-1, keepdims=True) + NORM_EPS) * dnw_ref[...]
        parts.append((seg * sz_ref[:, cols]).astype(BF16))
    mixed = jnp.dot(jnp.concatenate(parts, axis=1), wo_ref[...], preferred_element_type=F32)
    x = x_ref[...] + mod_ref[2:3, :] * mixed

    y = x * lax.rsqrt(jnp.mean(x * x, axis=-1, keepdims=True) + NORM_EPS) * n2w_ref[...]
    h2 = (y * (1.0 + mod_ref[4:5, :]) + mod_ref[3:4, :]).astype(BF16)
    acc = None
    for c in range(w1_ref.shape[1] // ff_tile):
        a = jnp.dot(h2, w1_ref[:, c * ff_tile:(c + 1) * ff_tile], preferred_element_type=F32)
        a = jnp.square(jnp.maximum(a, 0.0)).astype(BF16)
        part = jnp.dot(a, w2_ref[c * ff_tile:(c + 1) * ff_tile, :], preferred_element_type=F32)
        acc = part if acc is None else acc + part
    o_ref[...] = x + mod_ref[5:6, :] * acc


def _out_mlp(x, o_attn, o_f, o_b, sz, mod, dnw, w_out, n2w, w1, w2, tm, ff_tile):
    b, t, d = x.shape
    row = lambda bi, i: (bi, i, 0)
    full = lambda bi, i: (0, 0)
    resident = lambda w: pl.BlockSpec(w.shape, full, pipeline_mode=pl.Buffered(1))
    return pl.pallas_call(
        functools.partial(_out_mlp_kernel, ff_tile),
        grid=(b, t // tm),
        in_specs=[pl.BlockSpec((None, tm, d), row),
                  pl.BlockSpec((None, tm, ATTN_W), row),
                  pl.BlockSpec((None, tm, DN_W), row),
                  pl.BlockSpec((None, tm, DN_W), row),
                  pl.BlockSpec((None, tm, DN_W), row),
                  pl.BlockSpec((None, N_MOD, d), lambda bi, i: (bi, 0, 0)),
                  pl.BlockSpec((1, DN_HEAD_DIM), full),
                  resident(w_out),
                  pl.BlockSpec((1, d), full),
                  resident(w1),
                  resident(w2)],
        out_specs=pl.BlockSpec((None, tm, d), row),
        out_shape=jax.ShapeDtypeStruct((b, t, d), F32),
        compiler_params=_cparams(("parallel", "parallel")),
        name="out_mlp",
    )(x, o_attn, o_f, o_b, sz, mod, dnw, w_out, n2w, w1, w2)


def _rope_tables(t):
    half = ATTN_HEAD_DIM // 4
    inv = ROPE_THETA ** (-jnp.arange(half, dtype=F32) / half)
    pos = jnp.arange(t, dtype=jnp.int32)
    ang_r = (pos // GRID_W).astype(F32)[:, None] * inv[None, :]
    ang_c = (pos % GRID_W).astype(F32)[:, None] * inv[None, :]
    cos = jnp.concatenate([jnp.cos(ang_r)] * 2 + [jnp.cos(ang_c)] * 2, axis=1)
    sin = jnp.concatenate([-jnp.sin(ang_r), jnp.sin(ang_r), -jnp.sin(ang_c), jnp.sin(ang_c)], axis=1)
    rep = LANES // ATTN_HEAD_DIM
    return jnp.tile(cos, (1, rep)), jnp.tile(sin, (1, rep))


def _pad_lanes(vec, offset):
    return jnp.zeros((1, GATE_W), F32).at[0, offset:offset + vec.size].set(vec.reshape(-1).astype(F32))


def kernel(x, c, ctx, c_ctx, w_mod, b_mod, norm1_w, w_in, q_norm_w, k_norm_w, conv_w, a_log, dt_bias,
           dn_norm_w, w_out, norm2_w, w_mlp1, w_mlp2):
    b, t, d = x.shape
    n_ctx = ctx.shape[1]
    depth = w_mod.shape[0]
    cos, sin = _rope_tables(t)
    head_id = jnp.arange(ATTN_W, dtype=jnp.int32) // ATTN_HEAD_DIM
    gmat = (head_id[:, None] == head_id[None, :]).astype(BF16)
    c_rows = jnp.zeros((8, d), F32).at[:b].set(c).at[b].set(c_ctx)
    tm = min(1024, t)
    tm_out = min(512, t)
    for l in range(depth):
        mod = _modulation(c_rows, w_mod, b_mod[l], l).reshape(8, N_MOD, d)
        w_in_p = jnp.pad(w_in[l], ((0, 0), (0, IN_COLS_PAD - w_in.shape[2]))).astype(BF16)
        consts = (norm1_w[l].reshape(1, d), w_in_p,
                  jnp.tile(q_norm_w[l], ATTN_HEADS).reshape(1, ATTN_W),
                  jnp.tile(k_norm_w[l], ATTN_KV_HEADS).reshape(1, ATTN_KV_W),
                  cos, sin, gmat, _pad_lanes(a_log[l], N_GATE), _pad_lanes(dt_bias[l], N_GATE))
        q_a, k_a, v_a, dqkv, sz, gates, gates_t = _inproj(
            x, mod, lambda bi, i: (bi, 0, 0), False, tm, consts + (conv_w[l],))
        k_c, v_c, ckv, cgates, cgates_t = _inproj(
            ctx, mod, lambda bi, i: (b, 0, 0), True, n_ctx, consts + (conv_w[l][:, DN_W:],))

        attn_args = (q_a, k_a, v_a, k_c, v_c, min(1024, t), min(2048, t))
        o_fixed, bad, qkv = _attention(False, *attn_args, dqkv=dqkv, conv_w=conv_w[l])
        o_attn = lax.cond(jnp.max(bad) > 0.0, lambda: _attention(True, *attn_args)[0], lambda: o_fixed)

        s_zero = jnp.zeros((b * N_GATE, DN_HEAD_DIM, DN_HEAD_DIM), F32)
        (s_ctx,) = _gdn_scan(ckv, cgates, cgates_t, s_zero, False, 4)
        o_f, o_b, _ = _gdn_scan(qkv, gates, gates_t, s_ctx, True, 4)

        x = _out_mlp(x, o_attn, o_f, o_b, sz, mod, dn_norm_w[l].reshape(1, DN_HEAD_DIM),
                     w_out[l].astype(BF16), norm2_w[l].reshape(1, d), w_mlp1[l].astype(BF16),
                     w_mlp2[l].astype(BF16), tm_out, 1024)
    return x
```

```python
import functools

import jax
import jax.numpy as jnp
from jax import lax
from jax.experimental import pallas as pl
from jax.experimental.pallas import tpu as pltpu

F32 = jnp.float32
BF16 = jnp.bfloat16

GRID_W = 64
ATTN_HEAD_DIM = 64
ATTN_HEADS = 8
ATTN_KV_HEADS = 2
ATTN_GROUP = ATTN_HEADS // ATTN_KV_HEADS
ATTN_W = ATTN_HEADS * ATTN_HEAD_DIM
ATTN_KV_W = ATTN_KV_HEADS * ATTN_HEAD_DIM
DN_HEAD_DIM = 128
DN_HEADS = 4
DN_W = DN_HEADS * DN_HEAD_DIM
N_DIR = 2
CONV_K = 5
CHUNK = 64
N_MOD = 6
ROPE_THETA = 10000.0
NORM_EPS = 1e-6
LANES = 128
GATE_W = LANES
N_GATE = N_DIR * DN_HEADS

OFF_Q = 0
OFF_K = OFF_Q + ATTN_W
OFF_V = OFF_K + ATTN_KV_W
OFF_DQKV = OFF_V + ATTN_KV_W
OFF_Z = OFF_DQKV + 3 * DN_W
OFF_GATE = OFF_Z + DN_W
IN_COLS_PAD = OFF_GATE + GATE_W

SUBLANES = 8
HALO = SUBLANES
ROPE_HALF = ATTN_HEAD_DIM // 4
Q_EXT_W = ATTN_HEADS * LANES
V_EXT_W = 2 * LANES

VMEM_LIMIT = 56 * 1024 * 1024

MOD_COLS = 1024
INPROJ_ROWS = 1024
ATTN_Q_ROWS = 1024
ATTN_K_ROWS = 2048
GDN_CHUNKS = 8
OUT_ROWS = 512
MLP_FF_COLS = 1024


def _cparams(sem):
    return pltpu.CompilerParams(dimension_semantics=sem, vmem_limit_bytes=VMEM_LIMIT)


def _dot_bf16(a, b):
    return jnp.dot(a.astype(BF16), b.astype(BF16), preferred_element_type=F32)


def _dot_f32(a, b):
    return jnp.dot(a, b, preferred_element_type=F32, precision=lax.Precision.HIGHEST)


def _silu(x):
    return x * jax.nn.sigmoid(x)


def _mod_kernel(c_ref, w_ref, b_ref, o_ref):
    o_ref[...] = _dot_f32(_silu(c_ref[...]), w_ref[...]) + b_ref[...]


def _modulation(c_rows, w_mod, b_mod, layer):
    r, d = c_rows.shape
    n = w_mod.shape[2]
    tn = MOD_COLS
    return pl.pallas_call(
        _mod_kernel,
        grid=(n // tn,),
        in_specs=[pl.BlockSpec((r, d), lambda j: (0, 0)),
                  pl.BlockSpec((None, d, tn), lambda j: (layer, 0, j)),
                  pl.BlockSpec((1, tn), lambda j: (0, j))],
        out_specs=pl.BlockSpec((r, tn), lambda j: (0, j)),
        out_shape=jax.ShapeDtypeStruct((r, n), F32),
        compiler_params=_cparams(("arbitrary",)),
        name="modulation",
    )(c_rows, w_mod, b_mod.reshape(1, n))


def _group_mean_sq(a, gmat, group):
    return jnp.dot((a * a).astype(BF16), gmat, preferred_element_type=F32) * (1.0 / group)


def _rope(a, cos, sin):
    parts = []
    for c in range(a.shape[1] // LANES):
        ac = a[:, c * LANES:(c + 1) * LANES]
        lane = lax.broadcasted_iota(jnp.int32, ac.shape, 1)
        nxt = pltpu.roll(ac, LANES - ROPE_HALF, 1)
        prv = pltpu.roll(ac, ROPE_HALF, 1)
        parts.append(jnp.where((lane & (2 * ROPE_HALF - 1)) < ROPE_HALF, nxt, prv))
    partner = parts[0] if len(parts) == 1 else jnp.concatenate(parts, axis=1)
    return a * cos + partner * sin


def _chunk_scans(x):
    n = x.shape[0]
    pos = lax.broadcasted_iota(jnp.int32, x.shape, 0) & (CHUNK - 1)
    fwd, rev = x, x
    s = 1
    while s < CHUNK:
        fwd = fwd + jnp.where(pos >= s, pltpu.roll(fwd, s, 0), 0.0)
        rev = rev + jnp.where(pos + s < CHUNK, pltpu.roll(rev, n - s, 0), 0.0)
        s *= 2
    return fwd, rev


def _gate_maps(raw, alog, dtb):
    lane = lax.broadcasted_iota(jnp.int32, raw.shape, 1)
    beta = jax.nn.sigmoid(raw)
    y = raw + dtb
    softplus = jnp.maximum(y, 0.0) + jnp.log(1.0 + jnp.exp(-jnp.abs(y)))
    g = -jnp.exp(alog) * softplus
    fwd, rev = _chunk_scans(g)
    gc = jnp.where(lane < N_GATE + DN_HEADS, fwd, rev)
    total = fwd + rev - g
    e_gc = pltpu.roll(jnp.exp(gc), N_GATE, 1)
    e_dec = pltpu.roll(jnp.exp(total - gc), 2 * N_GATE, 1)
    e_tot = pltpu.roll(jnp.exp(total), 3 * N_GATE, 1)
    out = jnp.where(lane < N_GATE, beta,
                    jnp.where(lane < 2 * N_GATE, gc,
                              jnp.where(lane < 3 * N_GATE, e_gc,
                                        jnp.where(lane < 4 * N_GATE, e_dec, e_tot))))
    return out, gc


def _conv_silu_norm(win_ref, w_ref, o_ref, n_norm, groups=None):
    tm = o_ref.shape[0]
    for hd in (range(o_ref.shape[1] // LANES) if groups is None else groups):
        cols = slice(hd * LANES, (hd + 1) * LANES)
        win = win_ref[:, cols]
        acc = None
        for j in range(CONV_K):
            shift = (CONV_K // 2 - j) % win.shape[0]
            tap = win if shift == 0 else pltpu.roll(win, shift, 0)
            term = tap[HALO:HALO + tm] * w_ref[j:j + 1, cols]
            acc = term if acc is None else acc + term
        y = _silu(acc)
        if hd < n_norm:
            y = y * lax.rsqrt(jnp.sum(y * y, axis=-1, keepdims=True) + NORM_EPS)
        o_ref[:, cols] = y


def _inproj_kernel(is_ctx, x_ref, mod_ref, n1w_ref, w_ref, qnw_ref, knw_ref, cos_ref, sin_ref,
                   gmat_ref, alog_ref, dtb_ref, cw_ref, *refs):
    x = x_ref[...]
    tm = x.shape[0]
    y = x * lax.rsqrt(jnp.mean(x * x, axis=-1, keepdims=True) + NORM_EPS) * n1w_ref[...]
    h = (y * (1.0 + mod_ref[1:2, :]) + mod_ref[0:1, :]).astype(BF16)

    def proj(lo, hi):
        return jnp.dot(h, w_ref[:, lo:hi], preferred_element_type=F32)

    kv = proj(OFF_K, OFF_DQKV)
    kk = kv[:, :ATTN_KV_W]
    kk = kk * lax.rsqrt(_group_mean_sq(kk, gmat_ref[:ATTN_KV_W, :ATTN_KV_W], ATTN_HEAD_DIM)
                        + NORM_EPS) * knw_ref[...]
    vv = kv[:, ATTN_KV_W:]
    gates, gc = _gate_maps(proj(OFF_GATE, IN_COLS_PAD), alog_ref[...], dtb_ref[...])
    if is_ctx:
        k_out, v_out, conv_out, gate_out, gt_out, win_ref = refs
        edge = jnp.zeros((HALO, conv_out.shape[1]), F32)
        win_ref[0:HALO, :] = edge
        win_ref[HALO:HALO + tm, :] = proj(OFF_DQKV + DN_W, OFF_Z)
        win_ref[HALO + tm:, :] = edge
        _conv_silu_norm(win_ref, cw_ref, conv_out, DN_HEADS)
    else:
        q_out, k_out, v_out, dqkv_out, sz_out, gate_out, gt_out = refs
        dqkv_out[...] = proj(OFF_DQKV, OFF_Z)
        cos = cos_ref[...]
        sin = sin_ref[...]
        kk = _rope(kk, cos, sin)
        qq = proj(OFF_Q, OFF_K)
        qq = qq * lax.rsqrt(_group_mean_sq(qq, gmat_ref[...], ATTN_HEAD_DIM) + NORM_EPS) * qnw_ref[...]
        rep = ATTN_W // LANES
        qq = _rope(qq, jnp.concatenate([cos] * rep, axis=1), jnp.concatenate([sin] * rep, axis=1))
        qb = (qq * (ATTN_HEAD_DIM ** -0.5)).astype(q_out.dtype)
        zeros = jnp.zeros((qb.shape[0], ATTN_HEAD_DIM), q_out.dtype)
        for hd in range(ATTN_HEADS):
            seg = qb[:, hd * ATTN_HEAD_DIM:(hd + 1) * ATTN_HEAD_DIM]
            pair = [seg, zeros] if hd // ATTN_GROUP == 0 else [zeros, seg]
            q_out[:, hd * LANES:(hd + 1) * LANES] = jnp.concatenate(pair, axis=1)
        sz_out[...] = _silu(proj(OFF_Z, OFF_GATE))
    k_out[...] = kk.astype(k_out.dtype)
    v_out[:, :ATTN_KV_W] = vv.astype(v_out.dtype)
    v_out[:, ATTN_KV_W:] = jnp.ones((vv.shape[0], V_EXT_W - ATTN_KV_W), v_out.dtype)
    gate_out[...] = gates
    gc_t = gc.T
    for ch in range(gt_out.shape[0]):
        gt_out[ch] = gc_t[N_GATE:2 * N_GATE, ch * CHUNK:(ch + 1) * CHUNK]


def _inproj(x, mod, mod_index, is_ctx, tm, consts):
    b, t, d = x.shape
    n1w, w_in, qnw, knw, cos, sin, gmat, alog, dtb, conv_w = consts
    nt = t // tm
    assert not is_ctx or nt == 1, "the context call convolves in-kernel and needs the whole sequence in one tile"
    row = lambda bi, i: (bi, i, 0)
    full = lambda bi, i: (0, 0)
    conv_cols = conv_w.shape[1]
    in_specs = [
        pl.BlockSpec((None, tm, d), row),
        pl.BlockSpec((None, N_MOD, d), mod_index),
        pl.BlockSpec((1, d), full),
        pl.BlockSpec(w_in.shape, full),
        pl.BlockSpec(qnw.shape, full),
        pl.BlockSpec(knw.shape, full),
        pl.BlockSpec((tm, LANES), lambda bi, i: (i, 0)),
        pl.BlockSpec((tm, LANES), lambda bi, i: (i, 0)),
        pl.BlockSpec(gmat.shape, full),
        pl.BlockSpec((1, GATE_W), full),
        pl.BlockSpec((1, GATE_W), full),
        pl.BlockSpec(conv_w.shape, full),
    ]

    def out(width, dtype):
        return pl.BlockSpec((None, tm, width), row), jax.ShapeDtypeStruct((b, t, width), dtype)

    if is_ctx:
        outs = [out(ATTN_KV_W, BF16), out(V_EXT_W, BF16), out(conv_cols, F32), out(GATE_W, F32)]
    else:
        outs = [out(Q_EXT_W, BF16), out(ATTN_KV_W, BF16), out(V_EXT_W, BF16),
                out(conv_cols, F32), out(DN_W, F32), out(GATE_W, F32)]
    outs.append((pl.BlockSpec((None, tm // CHUNK, N_GATE, CHUNK), lambda bi, i: (bi, i, 0, 0)),
                 jax.ShapeDtypeStruct((b, t // CHUNK, N_GATE, CHUNK), F32)))
    return pl.pallas_call(
        functools.partial(_inproj_kernel, is_ctx),
        grid=(b, nt),
        in_specs=in_specs,
        out_specs=[o[0] for o in outs],
        out_shape=[o[1] for o in outs],
        scratch_shapes=[pltpu.VMEM((tm + 2 * HALO, conv_cols), F32)] if is_ctx else [],
        compiler_params=_cparams(("parallel", "parallel")),
        name="inproj_ctx" if is_ctx else "inproj",
    )(x, mod, n1w, w_in, qnw, knw, cos, sin, gmat, alog, dtb, conv_w)


def _attn_kernel(online, q_ref, kc_ref, vc_ref, k_ref, v_ref, *refs):
    if online:
        o_ref, bad_ref, m_ref, acc_ref = refs
    else:
        d_ref, dp_ref, dn_ref, cw_ref, o_ref, bad_ref, conv_ref, m_ref, acc_ref, win_ref = refs
    ki = pl.program_id(2)
    nk = pl.num_programs(2)

    def block(kb, vb, first, side_job=None):
        reps = kb.shape[0] // LANES

        def scores(h):
            return lax.dot_general(q_ref[:, h * LANES:(h + 1) * LANES], kb, (((1,), (1,)), ((), ())),
                                   preferred_element_type=F32)

        s_next = scores(0)
        for h in range(ATTN_HEADS):
            s = s_next
            if h + 1 < ATTN_HEADS:
                s_next = scores(h + 1)
            if first:
                m = jnp.broadcast_to(jnp.max(s, axis=-1, keepdims=True), m_ref.shape[1:])
                m_ref[h] = m
            elif online:
                m_prev = m_ref[h]
                m = jnp.maximum(m_prev, jnp.max(s, axis=-1, keepdims=True))
                alpha = jnp.exp(m_prev - m)
                m_ref[h] = m
            else:
                m = m_ref[h]
            p = jnp.exp(s - jnp.concatenate([m] * reps, axis=1))
            pv = jnp.dot(p.astype(BF16), vb, preferred_element_type=F32)
            if first:
                acc_ref[h] = pv
            elif online:
                acc_ref[h] = jnp.concatenate([alpha] * (V_EXT_W // LANES), axis=1) * acc_ref[h] + pv
            else:
                acc_ref[h] += pv
            if side_job is not None:
                side_job(h)

    @pl.when(ki == 0)
    def _():
        block(kc_ref[...], vc_ref[...], True)

    if online:
        block(k_ref[...], v_ref[...], False)
    else:
        tile = pl.program_id(1) * nk + ki
        n_tiles = pl.num_programs(1) * nk
        ct = d_ref.shape[0]
        win_ref[0:HALO, :] = jnp.where(tile > 0, dp_ref[...], 0.0)
        win_ref[HALO:HALO + ct, :] = d_ref[...]
        win_ref[HALO + ct:, :] = jnp.where(tile < n_tiles - 1, dn_ref[...], 0.0)
        n_groups = conv_ref.shape[1] // LANES

        def conv_share(h):
            groups = range(h * n_groups // ATTN_HEADS, (h + 1) * n_groups // ATTN_HEADS)
            _conv_silu_norm(win_ref, cw_ref, conv_ref, 2 * DN_HEADS, groups)

        block(k_ref[...], v_ref[...], False, conv_share)

    @pl.when(ki == nk - 1)
    def _():
        bad = jnp.zeros((1, 1), F32)
        for h in range(ATTN_HEADS):
            j = h // ATTN_GROUP
            acc = acc_ref[h]
            num = acc[:, j * ATTN_HEAD_DIM:(j + 1) * ATTN_HEAD_DIM]
            den = acc[:, ATTN_KV_W + j * ATTN_HEAD_DIM:ATTN_KV_W + (j + 1) * ATTN_HEAD_DIM]
            out = num / den
            o_ref[:, h * ATTN_HEAD_DIM:(h + 1) * ATTN_HEAD_DIM] = out.astype(o_ref.dtype)
            bad = jnp.maximum(bad, jnp.max(jnp.where(jnp.isfinite(out), 0.0, 1.0), keepdims=True))
        bad_ref[...] = jnp.broadcast_to(bad, bad_ref.shape)


def _attention(online, q, k, v, kc, vc, tq, tk, dqkv=None, conv_w=None):
    b, t, _ = q.shape
    n_ctx = kc.shape[1]
    nq, nk = t // tq, t // tk
    in_specs = [
        pl.BlockSpec((None, tq, Q_EXT_W), lambda bi, qi, ki: (bi, qi, 0)),
        pl.BlockSpec((None, n_ctx, ATTN_KV_W), lambda bi, qi, ki: (bi, 0, 0)),
        pl.BlockSpec((None, n_ctx, V_EXT_W), lambda bi, qi, ki: (bi, 0, 0)),
        pl.BlockSpec((None, tk, ATTN_KV_W), lambda bi, qi, ki: (bi, ki, 0)),
        pl.BlockSpec((None, tk, V_EXT_W), lambda bi, qi, ki: (bi, ki, 0)),
    ]
    out_specs = [pl.BlockSpec((None, tq, ATTN_W), lambda bi, qi, ki: (bi, qi, 0)),
                 pl.BlockSpec((None, None, SUBLANES, LANES), lambda bi, qi, ki: (bi, qi, 0, 0))]
    out_shape = [jax.ShapeDtypeStruct((b, t, ATTN_W), BF16),
                 jax.ShapeDtypeStruct((b, nq, SUBLANES, LANES), F32)]
    scratch = [pltpu.VMEM((ATTN_HEADS, tq, LANES), F32),
               pltpu.VMEM((ATTN_HEADS, tq, V_EXT_W), F32)]
    args = (q, kc, vc, k, v)
    if not online:
        c = dqkv.shape[2]
        ct = t // (nq * nk)
        assert ct % HALO == 0 and ct * nq * nk == t
        per = ct // HALO
        last = t // HALO - 1
        tile = lambda qi, ki: qi * nk + ki
        in_specs += [
            pl.BlockSpec((None, ct, c), lambda bi, qi, ki: (bi, tile(qi, ki), 0)),
            pl.BlockSpec((None, HALO, c), lambda bi, qi, ki: (bi, jnp.maximum(tile(qi, ki) * per - 1, 0), 0)),
            pl.BlockSpec((None, HALO, c), lambda bi, qi, ki: (bi, jnp.minimum((tile(qi, ki) + 1) * per, last), 0)),
            pl.BlockSpec(conv_w.shape, lambda bi, qi, ki: (0, 0)),
        ]
        out_specs.append(pl.BlockSpec((None, ct, c), lambda bi, qi, ki: (bi, tile(qi, ki), 0)))
        out_shape.append(jax.ShapeDtypeStruct((b, t, c), F32))
        scratch.append(pltpu.VMEM((ct + 2 * HALO, c), F32))
        args += (dqkv, dqkv, dqkv, conv_w)
    return pl.pallas_call(
        functools.partial(_attn_kernel, online),
        grid=(b, nq, nk),
        in_specs=in_specs,
        out_specs=out_specs,
        out_shape=out_shape,
        scratch_shapes=scratch,
        compiler_params=_cparams(("parallel", "parallel", "arbitrary")),
        name="attention_online" if online else "attention",
    )(*args)


def _dot_nt(a, b):
    return lax.dot_general(a, b, (((1,), (1,)), ((), ())), preferred_element_type=F32)


def _dot_tn(a, b):
    return lax.dot_general(a, b, (((0,), (0,)), ((), ())), preferred_element_type=F32)


def _unit_tri_inverses(ns, lower, row, col):
    eye = (row == col).astype(F32)
    ds = None
    s = 1
    while s < CHUNK:
        sh = s.bit_length() - 1
        join = ((row >> (sh + 1)) == (col >> (sh + 1))) & ((row >> sh) != (col >> sh))
        if s == 1:
            ds = [eye - jnp.where(join, n, 0.0) for n in ns]
        elif s < SUBLANES:
            es = [jnp.where(join, n, 0.0) for n in ns]
            eds = [_dot_bf16(e, d) for e, d in zip(es, ds)]
            ds = [d - _dot_bf16(d, ed) for d, ed in zip(ds, eds)]
        else:
            strips = CHUNK // s

            def active(a, low):
                return jnp.concatenate([a[i * s:(i + 1) * s] for i in range(strips) if (i % 2 == 1) == low],
                                       axis=0)

            zero = jnp.zeros((s, CHUNK), F32)
            eds = []
            for n, d, low in zip(ns, ds, lower):
                e_act = jnp.where(active(join, low), active(n, low), 0.0)
                ed = _dot_bf16(e_act, d)
                parts = [zero if (i % 2 == 1) != low else ed[(i // 2) * s:(i // 2 + 1) * s]
                         for i in range(strips)]
                eds.append(jnp.concatenate(parts, axis=0))
            new_ds = []
            for d, ed, low in zip(ds, eds, lower):
                upd = _dot_bf16(active(d, low), ed)
                parts = [d[i * s:(i + 1) * s] if (i % 2 == 1) != low else
                         d[i * s:(i + 1) * s] - upd[(i // 2) * s:(i // 2 + 1) * s] for i in range(strips)]
                new_ds.append(jnp.concatenate(parts, axis=0))
            ds = new_ds
        s *= 2
    return ds


def _gdn_chunk(has_q, nb, sub, n_sub, xf_ref, xb_ref, gf_ref, gb_ref, tf_ref, tb_ref, of_ref, ob_ref, s_ref):
    row = lax.broadcasted_iota(jnp.int32, (CHUNK, CHUNK), 0)
    col = lax.broadcasted_iota(jnp.int32, (CHUNK, CHUNK), 1)
    off_k = DN_W if has_q else 0
    off_v = off_k + DN_W
    scale = DN_HEAD_DIM ** -0.5
    units = [(bi, d, h) for bi in range(nb) for d in range(N_DIR) for h in range(DN_HEADS)]

    ops = []
    for bi, d, h in units:
        x_ref, g_ref, t_ref = (xf_ref, gf_ref, tf_ref) if d == 0 else (xb_ref, gb_ref, tb_ref)
        c = sub if d == 0 else n_sub - 1 - sub
        rows = slice(c * CHUNK, (c + 1) * CHUNK)
        incl = (row >= col) if d == 0 else (row <= col)
        strict = (row > col) if d == 0 else (row < col)
        u = d * DN_HEADS + h

        def gate_col(kind):
            return g_ref[bi, rows, kind * N_GATE + u:kind * N_GATE + u + 1]

        beta, gc, e_gc, e_dec = gate_col(0), gate_col(1), gate_col(2), gate_col(3)
        e_tot = g_ref[bi, c * CHUNK:c * CHUNK + 1, 4 * N_GATE + u:4 * N_GATE + u + 1]
        gc_t = t_ref[bi, c, u:u + 1, :]
        head = lambda off: x_ref[bi, rows, off + h * DN_HEAD_DIM:off + (h + 1) * DN_HEAD_DIM]
        k, v = head(off_k), head(off_v)
        k16 = k.astype(BF16)
        kb = k * beta
        decay = jnp.exp(jnp.where(incl, gc - gc_t, -jnp.inf))
        if has_q:
            q = head(0) * scale
            kq = _dot_nt(jnp.concatenate([kb, q], axis=0).astype(BF16), k16)
            kk, qk = kq[:CHUNK], kq[CHUNK:]
            intra = (qk * decay).astype(BF16)
            qd = (q * e_gc).astype(BF16)
        else:
            kk = _dot_nt(kb.astype(BF16), k16)
            intra = qd = None
        ops.append(dict(
            n=jnp.where(strict, kk * decay, 0.0),
            rhs=jnp.concatenate([v * beta, kb * e_gc], axis=1).astype(BF16),
            k_dec=(k * e_dec).astype(BF16), e_tot=e_tot, intra=intra, qd=qd, rows=rows))

    t_invs = _unit_tri_inverses([o["n"] for o in ops], [d == 0 for _, d, _ in units], row, col)
    uws = [jnp.dot(t.astype(BF16), o["rhs"], preferred_element_type=F32) for t, o in zip(t_invs, ops)]

    reads = []
    for idx, (o, uw) in enumerate(zip(ops, uws)):
        s_old = s_ref[idx]
        w = uw[:, DN_HEAD_DIM:].astype(BF16)
        lhs = jnp.concatenate([w, o["qd"]], axis=0) if has_q else w
        reads.append((s_old, jnp.dot(lhs, s_old.astype(BF16), preferred_element_type=F32)))

    for idx, ((bi, d, h), o, uw, (s_old, rd)) in enumerate(zip(units, ops, uws, reads)):
        v_new = (uw[:, :DN_HEAD_DIM] - rd[:CHUNK]).astype(BF16)
        s_ref[idx] = s_old * o["e_tot"] + _dot_tn(o["k_dec"], v_new)
        if has_q:
            out = rd[CHUNK:] + jnp.dot(o["intra"], v_new, preferred_element_type=F32)
            o_ref = of_ref if d == 0 else ob_ref
            o_ref[bi, o["rows"], h * DN_HEAD_DIM:(h + 1) * DN_HEAD_DIM] = out


def _gdn_kernel(has_q, *refs):
    if has_q:
        (xf_ref, xb_ref, gf_ref, gb_ref, tf_ref, tb_ref, s0_ref, of_ref, ob_ref, sfin_ref, s_ref) = refs
    else:
        (xf_ref, xb_ref, gf_ref, gb_ref, tf_ref, tb_ref, s0_ref, sfin_ref, s_ref) = refs
        of_ref = ob_ref = None
    i = pl.program_id(0)
    n_steps = pl.num_programs(0)
    nb = xf_ref.shape[0]
    n_sub = xf_ref.shape[1] // CHUNK

    @pl.when(i == 0)
    def _():
        s_ref[...] = s0_ref[...]

    for sub in range(n_sub):
        _gdn_chunk(has_q, nb, sub, n_sub, xf_ref, xb_ref, gf_ref, gb_ref, tf_ref, tb_ref, of_ref, ob_ref, s_ref)

    @pl.when(i == n_steps - 1)
    def _():
        sfin_ref[...] = s_ref[...]


def _gdn_scan(x, gates, gates_t, s0, has_q, n_sub):
    b, t, c = x.shape
    rows = n_sub * CHUNK
    n_steps = t // rows
    n_units = b * N_GATE
    fwd = lambda i: (0, i, 0)
    bwd = lambda i: (0, n_steps - 1 - i, 0)
    state_spec = pl.BlockSpec((n_units, DN_HEAD_DIM, DN_HEAD_DIM), lambda i: (0, 0, 0))
    state_shape = jax.ShapeDtypeStruct((n_units, DN_HEAD_DIM, DN_HEAD_DIM), F32)
    out_specs = [state_spec]
    out_shape = [state_shape]
    if has_q:
        o_shape = jax.ShapeDtypeStruct((b, t, DN_W), F32)
        out_specs = [pl.BlockSpec((b, rows, DN_W), fwd), pl.BlockSpec((b, rows, DN_W), bwd)] + out_specs
        out_shape = [o_shape, o_shape] + out_shape
    return pl.pallas_call(
        functools.partial(_gdn_kernel, has_q),
        grid=(n_steps,),
        in_specs=[pl.BlockSpec((b, rows, c), fwd), pl.BlockSpec((b, rows, c), bwd),
                  pl.BlockSpec((b, rows, GATE_W), fwd), pl.BlockSpec((b, rows, GATE_W), bwd),
                  pl.BlockSpec((b, n_sub, N_GATE, CHUNK), lambda i: (0, i, 0, 0)),
                  pl.BlockSpec((b, n_sub, N_GATE, CHUNK), lambda i: (0, n_steps - 1 - i, 0, 0)),
                  state_spec],
        out_specs=out_specs,
        out_shape=out_shape,
        scratch_shapes=[pltpu.VMEM((n_units, DN_HEAD_DIM, DN_HEAD_DIM), F32)],
        compiler_params=_cparams(("arbitrary",)),
        name="gdn_scan" if has_q else "gdn_scan_ctx",
    )(x, x, gates, gates, gates_t, gates_t, s0)


def _out_mlp_kernel(ff_tile, x_ref, oa_ref, of_ref, ob_ref, sz_ref, mod_ref, dnw_ref, wo_ref, n2w_ref,
                    w1_ref, w2_ref, o_ref):
    o_dn = of_ref[...] + ob_ref[...]
    parts = [oa_ref[...]]
    for h in range(DN_HEADS):
        cols = slice(h * DN_HEAD_DIM, (h + 1) * DN_HEAD_DIM)
        seg = o_dn[:, cols]
        seg = seg * lax.rsqrt(jnp.mean(seg * seg, axis=-1, keepdims=True) + NORM_EPS) * dnw_ref[...]
        parts.append((seg * sz_ref[:, cols]).astype(BF16))
    mixed = jnp.dot(jnp.concatenate(parts, axis=1), wo_ref[...], preferred_element_type=F32)
    x = x_ref[...] + mod_ref[2:3, :] * mixed

    y = x * lax.rsqrt(jnp.mean(x * x, axis=-1, keepdims=True) + NORM_EPS) * n2w_ref[...]
    h2 = (y * (1.0 + mod_ref[4:5, :]) + mod_ref[3:4, :]).astype(BF16)
    acc = None
    for c in range(w1_ref.shape[1] // ff_tile):
        a = jnp.dot(h2, w1_ref[:, c * ff_tile:(c + 1) * ff_tile], preferred_element_type=F32)
        a = jnp.square(jnp.maximum(a, 0.0)).astype(BF16)
        part = jnp.dot(a, w2_ref[c * ff_tile:(c + 1) * ff_tile, :], preferred_element_type=F32)
        acc = part if acc is None else acc + part
    o_ref[...] = x + mod_ref[5:6, :] * acc


def _out_mlp(x, o_attn, o_f, o_b, sz, mod, dnw, w_out, n2w, w1, w2, tm, ff_tile):
    b, t, d = x.shape
    row = lambda bi, i: (bi, i, 0)
    full = lambda bi, i: (0, 0)
    resident = lambda w: pl.BlockSpec(w.shape, full, pipeline_mode=pl.Buffered(1))
    return pl.pallas_call(
        functools.partial(_out_mlp_kernel, ff_tile),
        grid=(b, t // tm),
        in_specs=[pl.BlockSpec((None, tm, d), row),
                  pl.BlockSpec((None, tm, ATTN_W), row),
                  pl.BlockSpec((None, tm, DN_W), row),
                  pl.BlockSpec((None, tm, DN_W), row),
                  pl.BlockSpec((None, tm, DN_W), row),
                  pl.BlockSpec((None, N_MOD, d), lambda bi, i: (bi, 0, 0)),
                  pl.BlockSpec((1, DN_HEAD_DIM), full),
                  resident(w_out),
                  pl.BlockSpec((1, d), full),
                  resident(w1),
                  resident(w2)],
        out_specs=pl.BlockSpec((None, tm, d), row),
        out_shape=jax.ShapeDtypeStruct((b, t, d), F32),
        compiler_params=_cparams(("parallel", "parallel")),
        name="out_mlp",
    )(x, o_attn, o_f, o_b, sz, mod, dnw, w_out, n2w, w1, w2)


def _rope_tables(t):
    inv = ROPE_THETA ** (-jnp.arange(ROPE_HALF, dtype=F32) / ROPE_HALF)
    pos = jnp.arange(t, dtype=jnp.int32)
    ang_r = (pos // GRID_W).astype(F32)[:, None] * inv[None, :]
    ang_c = (pos % GRID_W).astype(F32)[:, None] * inv[None, :]
    cos = jnp.concatenate([jnp.cos(ang_r)] * 2 + [jnp.cos(ang_c)] * 2, axis=1)
    sin = jnp.concatenate([-jnp.sin(ang_r), jnp.sin(ang_r), -jnp.sin(ang_c), jnp.sin(ang_c)], axis=1)
    rep = LANES // ATTN_HEAD_DIM
    return jnp.tile(cos, (1, rep)), jnp.tile(sin, (1, rep))


def _pad_lanes(vec, offset):
    return jnp.zeros((1, GATE_W), F32).at[0, offset:offset + vec.size].set(vec.reshape(-1).astype(F32))


def kernel(x, c, ctx, c_ctx, w_mod, b_mod, norm1_w, w_in, q_norm_w, k_norm_w, conv_w, a_log, dt_bias,
           dn_norm_w, w_out, norm2_w, w_mlp1, w_mlp2):
    b, t, d = x.shape
    n_ctx = ctx.shape[1]
    depth = w_mod.shape[0]
    cos, sin = _rope_tables(t)
    head_id = jnp.arange(ATTN_W, dtype=jnp.int32) // ATTN_HEAD_DIM
    gmat = (head_id[:, None] == head_id[None, :]).astype(BF16)
    assert b < SUBLANES, "the modulation rows (batch + the context row) share one sublane tile"
    c_rows = jnp.zeros((SUBLANES, d), F32).at[:b].set(c).at[b].set(c_ctx)
    tm, tm_out = min(INPROJ_ROWS, t), min(OUT_ROWS, t)
    tq, tk = min(ATTN_Q_ROWS, t), min(ATTN_K_ROWS, t)
    for l in range(depth):
        mod = _modulation(c_rows, w_mod, b_mod[l], l).reshape(SUBLANES, N_MOD, d)
        w_in_p = jnp.pad(w_in[l], ((0, 0), (0, IN_COLS_PAD - w_in.shape[2]))).astype(BF16)
        consts = (norm1_w[l].reshape(1, d), w_in_p,
                  jnp.tile(q_norm_w[l], ATTN_HEADS).reshape(1, ATTN_W),
                  jnp.tile(k_norm_w[l], ATTN_KV_HEADS).reshape(1, ATTN_KV_W),
                  cos, sin, gmat, _pad_lanes(a_log[l], N_GATE), _pad_lanes(dt_bias[l], N_GATE))
        q_a, k_a, v_a, dqkv, sz, gates, gates_t = _inproj(
            x, mod, lambda bi, i: (bi, 0, 0), False, tm, consts + (conv_w[l],))
        k_c, v_c, ckv, cgates, cgates_t = _inproj(
            ctx, mod, lambda bi, i: (b, 0, 0), True, n_ctx, consts + (conv_w[l][:, DN_W:],))

        attn_args = (q_a, k_a, v_a, k_c, v_c, tq, tk)
        o_fixed, bad, qkv = _attention(False, *attn_args, dqkv=dqkv, conv_w=conv_w[l])
        o_attn = lax.cond(jnp.max(bad) > 0.0, lambda: _attention(True, *attn_args)[0], lambda: o_fixed)

        s_zero = jnp.zeros((b * N_GATE, DN_HEAD_DIM, DN_HEAD_DIM), F32)
        (s_ctx,) = _gdn_scan(ckv, cgates, cgates_t, s_zero, False, min(GDN_CHUNKS, n_ctx // CHUNK))
        o_f, o_b, _ = _gdn_scan(qkv, gates, gates_t, s_ctx, True, min(GDN_CHUNKS, t // CHUNK))

        x = _out_mlp(x, o_attn, o_f, o_b, sz, mod, dn_norm_w[l].reshape(1, DN_HEAD_DIM),
                     w_out[l].astype(BF16), norm2_w[l].reshape(1, d), w_mlp1[l].astype(BF16),
                     w_mlp2[l].astype(BF16), tm_out, MLP_FF_COLS)
    return x
```

```python
import functools

import jax
import jax.numpy as jnp
from jax import lax
from jax.experimental import pallas as pl
from jax.experimental.pallas import tpu as pltpu

F32 = jnp.float32
BF16 = jnp.bfloat16

GRID_W = 64
ATTN_HEAD_DIM = 64
ATTN_HEADS = 8
ATTN_KV_HEADS = 2
ATTN_GROUP = ATTN_HEADS // ATTN_KV_HEADS
ATTN_W = ATTN_HEADS * ATTN_HEAD_DIM
ATTN_KV_W = ATTN_KV_HEADS * ATTN_HEAD_DIM
DN_HEAD_DIM = 128
DN_HEADS = 4
DN_W = DN_HEADS * DN_HEAD_DIM
N_DIR = 2
CONV_K = 5
CHUNK = 64
N_MOD = 6
ROPE_THETA = 10000.0
NORM_EPS = 1e-6
LANES = 128
GATE_W = LANES
N_GATE = N_DIR * DN_HEADS

OFF_Q = 0
OFF_K = OFF_Q + ATTN_W
OFF_V = OFF_K + ATTN_KV_W
OFF_DQKV = OFF_V + ATTN_KV_W
OFF_Z = OFF_DQKV + 3 * DN_W
OFF_GATE = OFF_Z + DN_W
IN_COLS_PAD = OFF_GATE + GATE_W

SUBLANES = 8
HALO = SUBLANES
ROPE_HALF = ATTN_HEAD_DIM // 4
Q_EXT_W = ATTN_HEADS * LANES
V_EXT_W = 2 * LANES

VMEM_LIMIT = 56 * 1024 * 1024

MOD_COLS = 512
INPROJ_ROWS = 1024
ATTN_Q_ROWS = 1024
ATTN_K_ROWS = 2048
GDN_CHUNKS = 8
OUT_ROWS = 512
MLP_FF_COLS = 1024


def _cparams(sem):
    return pltpu.CompilerParams(dimension_semantics=sem, vmem_limit_bytes=VMEM_LIMIT)


def _dot_bf16(a, b):
    return jnp.dot(a.astype(BF16), b.astype(BF16), preferred_element_type=F32)


def _dot_f32(a, b):
    return jnp.dot(a, b, preferred_element_type=F32, precision=lax.Precision.HIGHEST)


def _silu(x):
    return x * jax.nn.sigmoid(x)


def _mod_kernel(c_ref, w_ref, b_ref, o_ref):
    o_ref[...] = _dot_f32(_silu(c_ref[...]), w_ref[...]) + b_ref[...]


def _modulation(c_rows, w_mod, b_mod, layer):
    r, d = c_rows.shape
    n = w_mod.shape[2]
    tn = MOD_COLS
    return pl.pallas_call(
        _mod_kernel,
        grid=(n // tn,),
        in_specs=[pl.BlockSpec((r, d), lambda j: (0, 0)),
                  pl.BlockSpec((None, d, tn), lambda j: (layer, 0, j)),
                  pl.BlockSpec((1, tn), lambda j: (0, j))],
        out_specs=pl.BlockSpec((r, tn), lambda j: (0, j)),
        out_shape=jax.ShapeDtypeStruct((r, n), F32),
        compiler_params=_cparams(("arbitrary",)),
        name="modulation",
    )(c_rows, w_mod, b_mod.reshape(1, n))


def _group_mean_sq(a, gmat, group):
    return jnp.dot((a * a).astype(BF16), gmat, preferred_element_type=F32) * (1.0 / group)


def _rope(a, cos, sin):
    parts = []
    for c in range(a.shape[1] // LANES):
        ac = a[:, c * LANES:(c + 1) * LANES]
        lane = lax.broadcasted_iota(jnp.int32, ac.shape, 1)
        nxt = pltpu.roll(ac, LANES - ROPE_HALF, 1)
        prv = pltpu.roll(ac, ROPE_HALF, 1)
        parts.append(jnp.where((lane & (2 * ROPE_HALF - 1)) < ROPE_HALF, nxt, prv))
    partner = parts[0] if len(parts) == 1 else jnp.concatenate(parts, axis=1)
    return a * cos + partner * sin


def _chunk_scans(x):
    n = x.shape[0]
    pos = lax.broadcasted_iota(jnp.int32, x.shape, 0) & (CHUNK - 1)
    fwd, rev = x, x
    s = 1
    while s < CHUNK:
        fwd = fwd + jnp.where(pos >= s, pltpu.roll(fwd, s, 0), 0.0)
        rev = rev + jnp.where(pos + s < CHUNK, pltpu.roll(rev, n - s, 0), 0.0)
        s *= 2
    return fwd, rev


def _gate_maps(raw, alog, dtb):
    lane = lax.broadcasted_iota(jnp.int32, raw.shape, 1)
    beta = jax.nn.sigmoid(raw)
    y = raw + dtb
    softplus = jnp.maximum(y, 0.0) + jnp.log(1.0 + jnp.exp(-jnp.abs(y)))
    g = -jnp.exp(alog) * softplus
    fwd, rev = _chunk_scans(g)
    gc = jnp.where(lane < N_GATE + DN_HEADS, fwd, rev)
    total = fwd + rev - g
    e_gc = pltpu.roll(jnp.exp(gc), N_GATE, 1)
    e_dec = pltpu.roll(jnp.exp(total - gc), 2 * N_GATE, 1)
    e_tot = pltpu.roll(jnp.exp(total), 3 * N_GATE, 1)
    out = jnp.where(lane < N_GATE, beta,
                    jnp.where(lane < 2 * N_GATE, gc,
                              jnp.where(lane < 3 * N_GATE, e_gc,
                                        jnp.where(lane < 4 * N_GATE, e_dec, e_tot))))
    return out, gc


def _conv_silu_norm(win_ref, w_ref, o_ref, n_norm, groups=None):
    tm = o_ref.shape[0]
    for hd in (range(o_ref.shape[1] // LANES) if groups is None else groups):
        cols = slice(hd * LANES, (hd + 1) * LANES)
        win = win_ref[:, cols]
        acc = None
        for j in range(CONV_K):
            shift = (CONV_K // 2 - j) % win.shape[0]
            tap = win if shift == 0 else pltpu.roll(win, shift, 0)
            term = tap[HALO:HALO + tm] * w_ref[j:j + 1, cols]
            acc = term if acc is None else acc + term
        y = _silu(acc)
        if hd < n_norm:
            y = y * lax.rsqrt(jnp.sum(y * y, axis=-1, keepdims=True) + NORM_EPS)
        o_ref[:, cols] = y


def _inproj_kernel(is_ctx, x_ref, mod_ref, n1w_ref, w_ref, qnw_ref, knw_ref, cos_ref, sin_ref,
                   gmat_ref, alog_ref, dtb_ref, cw_ref, *refs):
    x = x_ref[...]
    tm = x.shape[0]
    y = x * lax.rsqrt(jnp.mean(x * x, axis=-1, keepdims=True) + NORM_EPS) * n1w_ref[...]
    h = (y * (1.0 + mod_ref[1:2, :]) + mod_ref[0:1, :]).astype(BF16)

    def proj(lo, hi):
        return jnp.dot(h, w_ref[:, lo:hi], preferred_element_type=F32)

    kv = proj(OFF_K, OFF_DQKV)
    kk = kv[:, :ATTN_KV_W]
    kk = kk * lax.rsqrt(_group_mean_sq(kk, gmat_ref[:ATTN_KV_W, :ATTN_KV_W], ATTN_HEAD_DIM)
                        + NORM_EPS) * knw_ref[...]
    vv = kv[:, ATTN_KV_W:]
    gates, gc = _gate_maps(proj(OFF_GATE, IN_COLS_PAD), alog_ref[...], dtb_ref[...])
    if is_ctx:
        k_out, v_out, conv_out, gate_out, gt_out, win_ref = refs
        edge = jnp.zeros((HALO, conv_out.shape[1]), F32)
        win_ref[0:HALO, :] = edge
        win_ref[HALO:HALO + tm, :] = proj(OFF_DQKV + DN_W, OFF_Z)
        win_ref[HALO + tm:, :] = edge
        _conv_silu_norm(win_ref, cw_ref, conv_out, DN_HEADS)
    else:
        q_out, k_out, v_out, dqkv_out, sz_out, gate_out, gt_out = refs
        dqkv_out[...] = proj(OFF_DQKV, OFF_Z)
        cos = cos_ref[...]
        sin = sin_ref[...]
        kk = _rope(kk, cos, sin)
        qq = proj(OFF_Q, OFF_K)
        qq = qq * lax.rsqrt(_group_mean_sq(qq, gmat_ref[...], ATTN_HEAD_DIM) + NORM_EPS) * qnw_ref[...]
        rep = ATTN_W // LANES
        qq = _rope(qq, jnp.concatenate([cos] * rep, axis=1), jnp.concatenate([sin] * rep, axis=1))
        qb = (qq * (ATTN_HEAD_DIM ** -0.5)).astype(q_out.dtype)
        zeros = jnp.zeros((qb.shape[0], ATTN_HEAD_DIM), q_out.dtype)
        for hd in range(ATTN_HEADS):
            seg = qb[:, hd * ATTN_HEAD_DIM:(hd + 1) * ATTN_HEAD_DIM]
            pair = [seg, zeros] if hd // ATTN_GROUP == 0 else [zeros, seg]
            q_out[:, hd * LANES:(hd + 1) * LANES] = jnp.concatenate(pair, axis=1)
        sz_out[...] = _silu(proj(OFF_Z, OFF_GATE))
    k_out[...] = kk.astype(k_out.dtype)
    v_out[:, :ATTN_KV_W] = vv.astype(v_out.dtype)
    v_out[:, ATTN_KV_W:] = jnp.ones((vv.shape[0], V_EXT_W - ATTN_KV_W), v_out.dtype)
    gate_out[...] = gates
    gc_t = gc.T
    for ch in range(gt_out.shape[0]):
        gt_out[ch] = gc_t[N_GATE:2 * N_GATE, ch * CHUNK:(ch + 1) * CHUNK]


def _inproj(x, mod, mod_index, is_ctx, tm, consts):
    b, t, d = x.shape
    n1w, w_in, qnw, knw, cos, sin, gmat, alog, dtb, conv_w = consts
    nt = t // tm
    assert not is_ctx or nt == 1, "the context call convolves in-kernel and needs the whole sequence in one tile"
    row = lambda bi, i: (bi, i, 0)
    full = lambda bi, i: (0, 0)
    conv_cols = conv_w.shape[1]
    in_specs = [
        pl.BlockSpec((None, tm, d), row),
        pl.BlockSpec((None, N_MOD, d), mod_index),
        pl.BlockSpec((1, d), full),
        pl.BlockSpec(w_in.shape, full),
        pl.BlockSpec(qnw.shape, full),
        pl.BlockSpec(knw.shape, full),
        pl.BlockSpec((tm, LANES), lambda bi, i: (i, 0)),
        pl.BlockSpec((tm, LANES), lambda bi, i: (i, 0)),
        pl.BlockSpec(gmat.shape, full),
        pl.BlockSpec((1, GATE_W), full),
        pl.BlockSpec((1, GATE_W), full),
        pl.BlockSpec(conv_w.shape, full),
    ]

    def out(width, dtype):
        return pl.BlockSpec((None, tm, width), row), jax.ShapeDtypeStruct((b, t, width), dtype)

    if is_ctx:
        outs = [out(ATTN_KV_W, BF16), out(V_EXT_W, BF16), out(conv_cols, F32), out(GATE_W, F32)]
    else:
        outs = [out(Q_EXT_W, BF16), out(ATTN_KV_W, BF16), out(V_EXT_W, BF16),
                out(conv_cols, F32), out(DN_W, F32), out(GATE_W, F32)]
    outs.append((pl.BlockSpec((None, tm // CHUNK, N_GATE, CHUNK), lambda bi, i: (bi, i, 0, 0)),
                 jax.ShapeDtypeStruct((b, t // CHUNK, N_GATE, CHUNK), F32)))
    return pl.pallas_call(
        functools.partial(_inproj_kernel, is_ctx),
        grid=(b, nt),
        in_specs=in_specs,
        out_specs=[o[0] for o in outs],
        out_shape=[o[1] for o in outs],
        scratch_shapes=[pltpu.VMEM((tm + 2 * HALO, conv_cols), F32)] if is_ctx else [],
        compiler_params=_cparams(("parallel", "parallel")),
        name="inproj_ctx" if is_ctx else "inproj",
    )(x, mod, n1w, w_in, qnw, knw, cos, sin, gmat, alog, dtb, conv_w)


def _attn_kernel(online, q_ref, kc_ref, vc_ref, k_ref, v_ref, *refs):
    if online:
        o_ref, bad_ref, m_ref, acc_ref = refs
    else:
        d_ref, dp_ref, dn_ref, cw_ref, o_ref, bad_ref, conv_ref, m_ref, acc_ref, win_ref = refs
    ki = pl.program_id(2)
    nk = pl.num_programs(2)

    def block(kb, vb, first, side_job=None):
        reps = kb.shape[0] // LANES

        def scores(h):
            return lax.dot_general(q_ref[:, h * LANES:(h + 1) * LANES], kb, (((1,), (1,)), ((), ())),
                                   preferred_element_type=F32)

        s_next = scores(0)
        for h in range(ATTN_HEADS):
            s = s_next
            if h + 1 < ATTN_HEADS:
                s_next = scores(h + 1)
            if first:
                m = jnp.broadcast_to(jnp.max(s, axis=-1, keepdims=True), m_ref.shape[1:])
                m_ref[h] = m
            elif online:
                m_prev = m_ref[h]
                m = jnp.maximum(m_prev, jnp.max(s, axis=-1, keepdims=True))
                alpha = jnp.exp(m_prev - m)
                m_ref[h] = m
            else:
                m = m_ref[h]
            p = jnp.exp(s - jnp.concatenate([m] * reps, axis=1))
            pv = jnp.dot(p.astype(BF16), vb, preferred_element_type=F32)
            if first:
                acc_ref[h] = pv
            elif online:
                acc_ref[h] = jnp.concatenate([alpha] * (V_EXT_W // LANES), axis=1) * acc_ref[h] + pv
            else:
                acc_ref[h] += pv
            if side_job is not None:
                side_job(h)

    @pl.when(ki == 0)
    def _():
        block(kc_ref[...], vc_ref[...], True)

    if online:
        block(k_ref[...], v_ref[...], False)
    else:
        tile = pl.program_id(1) * nk + ki
        n_tiles = pl.num_programs(1) * nk
        ct = d_ref.shape[0]
        win_ref[0:HALO, :] = jnp.where(tile > 0, dp_ref[...], 0.0)
        win_ref[HALO:HALO + ct, :] = d_ref[...]
        win_ref[HALO + ct:, :] = jnp.where(tile < n_tiles - 1, dn_ref[...], 0.0)
        n_groups = conv_ref.shape[1] // LANES

        def conv_share(h):
            groups = range(h * n_groups // ATTN_HEADS, (h + 1) * n_groups // ATTN_HEADS)
            _conv_silu_norm(win_ref, cw_ref, conv_ref, 2 * DN_HEADS, groups)

        block(k_ref[...], v_ref[...], False, conv_share)

    @pl.when(ki == nk - 1)
    def _():
        bad = jnp.zeros((1, 1), F32)
        for h in range(ATTN_HEADS):
            j = h // ATTN_GROUP
            acc = acc_ref[h]
            num = acc[:, j * ATTN_HEAD_DIM:(j + 1) * ATTN_HEAD_DIM]
            den = acc[:, ATTN_KV_W + j * ATTN_HEAD_DIM:ATTN_KV_W + (j + 1) * ATTN_HEAD_DIM]
            out = num / den
            o_ref[:, h * ATTN_HEAD_DIM:(h + 1) * ATTN_HEAD_DIM] = out.astype(o_ref.dtype)
            bad = jnp.maximum(bad, jnp.max(jnp.where(jnp.isfinite(out), 0.0, 1.0), keepdims=True))
        bad_ref[...] = jnp.broadcast_to(bad, bad_ref.shape)


def _attention(online, q, k, v, kc, vc, tq, tk, dqkv=None, conv_w=None):
    b, t, _ = q.shape
    n_ctx = kc.shape[1]
    nq, nk = t // tq, t // tk
    in_specs = [
        pl.BlockSpec((None, tq, Q_EXT_W), lambda bi, qi, ki: (bi, qi, 0)),
        pl.BlockSpec((None, n_ctx, ATTN_KV_W), lambda bi, qi, ki: (bi, 0, 0)),
        pl.BlockSpec((None, n_ctx, V_EXT_W), lambda bi, qi, ki: (bi, 0, 0)),
        pl.BlockSpec((None, tk, ATTN_KV_W), lambda bi, qi, ki: (bi, ki, 0)),
        pl.BlockSpec((None, tk, V_EXT_W), lambda bi, qi, ki: (bi, ki, 0)),
    ]
    out_specs = [pl.BlockSpec((None, tq, ATTN_W), lambda bi, qi, ki: (bi, qi, 0)),
                 pl.BlockSpec((None, None, SUBLANES, LANES), lambda bi, qi, ki: (bi, qi, 0, 0))]
    out_shape = [jax.ShapeDtypeStruct((b, t, ATTN_W), BF16),
                 jax.ShapeDtypeStruct((b, nq, SUBLANES, LANES), F32)]
    scratch = [pltpu.VMEM((ATTN_HEADS, tq, LANES), F32),
               pltpu.VMEM((ATTN_HEADS, tq, V_EXT_W), F32)]
    args = (q, kc, vc, k, v)
    if not online:
        c = dqkv.shape[2]
        ct = t // (nq * nk)
        assert ct % HALO == 0 and ct * nq * nk == t
        per = ct // HALO
        last = t // HALO - 1
        tile = lambda qi, ki: qi * nk + ki
        in_specs += [
            pl.BlockSpec((None, ct, c), lambda bi, qi, ki: (bi, tile(qi, ki), 0)),
            pl.BlockSpec((None, HALO, c), lambda bi, qi, ki: (bi, jnp.maximum(tile(qi, ki) * per - 1, 0), 0)),
            pl.BlockSpec((None, HALO, c), lambda bi, qi, ki: (bi, jnp.minimum((tile(qi, ki) + 1) * per, last), 0)),
            pl.BlockSpec(conv_w.shape, lambda bi, qi, ki: (0, 0)),
        ]
        out_specs.append(pl.BlockSpec((None, ct, c), lambda bi, qi, ki: (bi, tile(qi, ki), 0)))
        out_shape.append(jax.ShapeDtypeStruct((b, t, c), F32))
        scratch.append(pltpu.VMEM((ct + 2 * HALO, c), F32))
        args += (dqkv, dqkv, dqkv, conv_w)
    return pl.pallas_call(
        functools.partial(_attn_kernel, online),
        grid=(b, nq, nk),
        in_specs=in_specs,
        out_specs=out_specs,
        out_shape=out_shape,
        scratch_shapes=scratch,
        compiler_params=_cparams(("parallel", "parallel", "arbitrary")),
        name="attention_online" if online else "attention",
    )(*args)


def _dot_nt(a, b):
    return lax.dot_general(a, b, (((1,), (1,)), ((), ())), preferred_element_type=F32)


def _dot_tn(a, b):
    return lax.dot_general(a, b, (((0,), (0,)), ((), ())), preferred_element_type=F32)


def _unit_tri_inverses(ns, lower, row, col):
    eye = (row == col).astype(F32)
    ds = None
    s = 1
    while s < CHUNK:
        sh = s.bit_length() - 1
        join = ((row >> (sh + 1)) == (col >> (sh + 1))) & ((row >> sh) != (col >> sh))
        if s == 1:
            ds = [eye - jnp.where(join, n, 0.0) for n in ns]
        elif s < SUBLANES:
            es = [jnp.where(join, n, 0.0) for n in ns]
            eds = [_dot_bf16(e, d) for e, d in zip(es, ds)]
            ds = [d - _dot_bf16(d, ed) for d, ed in zip(ds, eds)]
        else:
            strips = CHUNK // s

            def active(a, low):
                return jnp.concatenate([a[i * s:(i + 1) * s] for i in range(strips) if (i % 2 == 1) == low],
                                       axis=0)

            zero = jnp.zeros((s, CHUNK), F32)
            eds = []
            for n, d, low in zip(ns, ds, lower):
                e_act = jnp.where(active(join, low), active(n, low), 0.0)
                ed = _dot_bf16(e_act, d)
                parts = [zero if (i % 2 == 1) != low else ed[(i // 2) * s:(i // 2 + 1) * s]
                         for i in range(strips)]
                eds.append(jnp.concatenate(parts, axis=0))
            new_ds = []
            for d, ed, low in zip(ds, eds, lower):
                upd = _dot_bf16(active(d, low), ed)
                parts = [d[i * s:(i + 1) * s] if (i % 2 == 1) != low else
                         d[i * s:(i + 1) * s] - upd[(i // 2) * s:(i // 2 + 1) * s] for i in range(strips)]
                new_ds.append(jnp.concatenate(parts, axis=0))
            ds = new_ds
        s *= 2
    return ds


def _gdn_chunk(has_q, nb, sub, n_sub, xf_ref, xb_ref, gf_ref, gb_ref, tf_ref, tb_ref, of_ref, ob_ref, s_ref):
    row = lax.broadcasted_iota(jnp.int32, (CHUNK, CHUNK), 0)
    col = lax.broadcasted_iota(jnp.int32, (CHUNK, CHUNK), 1)
    off_k = DN_W if has_q else 0
    off_v = off_k + DN_W
    scale = DN_HEAD_DIM ** -0.5
    units = [(bi, d, h) for bi in range(nb) for d in range(N_DIR) for h in range(DN_HEADS)]

    ops = []
    for bi, d, h in units:
        x_ref, g_ref, t_ref = (xf_ref, gf_ref, tf_ref) if d == 0 else (xb_ref, gb_ref, tb_ref)
        c = sub if d == 0 else n_sub - 1 - sub
        rows = slice(c * CHUNK, (c + 1) * CHUNK)
        incl = (row >= col) if d == 0 else (row <= col)
        strict = (row > col) if d == 0 else (row < col)
        u = d * DN_HEADS + h

        def gate_col(kind):
            return g_ref[bi, rows, kind * N_GATE + u:kind * N_GATE + u + 1]

        beta, gc, e_gc, e_dec = gate_col(0), gate_col(1), gate_col(2), gate_col(3)
        e_tot = g_ref[bi, c * CHUNK:c * CHUNK + 1, 4 * N_GATE + u:4 * N_GATE + u + 1]
        gc_t = t_ref[bi, c, u:u + 1, :]
        head = lambda off: x_ref[bi, rows, off + h * DN_HEAD_DIM:off + (h + 1) * DN_HEAD_DIM]
        k, v = head(off_k), head(off_v)
        k16 = k.astype(BF16)
        kb = k * beta
        decay = jnp.exp(jnp.where(incl, gc - gc_t, -jnp.inf))
        if has_q:
            q = head(0) * scale
            kq = _dot_nt(jnp.concatenate([kb, q], axis=0).astype(BF16), k16)
            kk, qk = kq[:CHUNK], kq[CHUNK:]
            intra = (qk * decay).astype(BF16)
            qd = (q * e_gc).astype(BF16)
        else:
            kk = _dot_nt(kb.astype(BF16), k16)
            intra = qd = None
        ops.append(dict(
            n=jnp.where(strict, kk * decay, 0.0),
            rhs=jnp.concatenate([v * beta, kb * e_gc], axis=1).astype(BF16),
            k_dec=(k * e_dec).astype(BF16), e_tot=e_tot, intra=intra, qd=qd, rows=rows))

    t_invs = _unit_tri_inverses([o["n"] for o in ops], [d == 0 for _, d, _ in units], row, col)
    uws = [jnp.dot(t.astype(BF16), o["rhs"], preferred_element_type=F32) for t, o in zip(t_invs, ops)]

    reads = []
    for idx, (o, uw) in enumerate(zip(ops, uws)):
        s_old = s_ref[idx]
        w = uw[:, DN_HEAD_DIM:].astype(BF16)
        lhs = jnp.concatenate([w, o["qd"]], axis=0) if has_q else w
        reads.append((s_old, jnp.dot(lhs, s_old.astype(BF16), preferred_element_type=F32)))

    for idx, ((bi, d, h), o, uw, (s_old, rd)) in enumerate(zip(units, ops, uws, reads)):
        v_new = (uw[:, :DN_HEAD_DIM] - rd[:CHUNK]).astype(BF16)
        s_ref[idx] = s_old * o["e_tot"] + _dot_tn(o["k_dec"], v_new)
        if has_q:
            out = rd[CHUNK:] + jnp.dot(o["intra"], v_new, preferred_element_type=F32)
            o_ref = of_ref if d == 0 else ob_ref
            o_ref[bi, o["rows"], h * DN_HEAD_DIM:(h + 1) * DN_HEAD_DIM] = out


def _gdn_kernel(has_q, *refs):
    if has_q:
        (xf_ref, xb_ref, gf_ref, gb_ref, tf_ref, tb_ref, s0_ref, of_ref, ob_ref, sfin_ref, s_ref) = refs
    else:
        (xf_ref, xb_ref, gf_ref, gb_ref, tf_ref, tb_ref, s0_ref, sfin_ref, s_ref) = refs
        of_ref = ob_ref = None
    i = pl.program_id(0)
    n_steps = pl.num_programs(0)
    nb = xf_ref.shape[0]
    n_sub = xf_ref.shape[1] // CHUNK

    @pl.when(i == 0)
    def _():
        s_ref[...] = s0_ref[...]

    for sub in range(n_sub):
        _gdn_chunk(has_q, nb, sub, n_sub, xf_ref, xb_ref, gf_ref, gb_ref, tf_ref, tb_ref, of_ref, ob_ref, s_ref)

    @pl.when(i == n_steps - 1)
    def _():
        sfin_ref[...] = s_ref[...]


def _gdn_scan(x, gates, gates_t, s0, has_q, n_sub):
    b, t, c = x.shape
    rows = n_sub * CHUNK
    n_steps = t // rows
    n_units = b * N_GATE
    fwd = lambda i: (0, i, 0)
    bwd = lambda i: (0, n_steps - 1 - i, 0)
    state_spec = pl.BlockSpec((n_units, DN_HEAD_DIM, DN_HEAD_DIM), lambda i: (0, 0, 0))
    state_shape = jax.ShapeDtypeStruct((n_units, DN_HEAD_DIM, DN_HEAD_DIM), F32)
    out_specs = [state_spec]
    out_shape = [state_shape]
    if has_q:
        o_shape = jax.ShapeDtypeStruct((b, t, DN_W), F32)
        out_specs = [pl.BlockSpec((b, rows, DN_W), fwd), pl.BlockSpec((b, rows, DN_W), bwd)] + out_specs
        out_shape = [o_shape, o_shape] + out_shape
    return pl.pallas_call(
        functools.partial(_gdn_kernel, has_q),
        grid=(n_steps,),
        in_specs=[pl.BlockSpec((b, rows, c), fwd), pl.BlockSpec((b, rows, c), bwd),
                  pl.BlockSpec((b, rows, GATE_W), fwd), pl.BlockSpec((b, rows, GATE_W), bwd),
                  pl.BlockSpec((b, n_sub, N_GATE, CHUNK), lambda i: (0, i, 0, 0)),
                  pl.BlockSpec((b, n_sub, N_GATE, CHUNK), lambda i: (0, n_steps - 1 - i, 0, 0)),
                  state_spec],
        out_specs=out_specs,
        out_shape=out_shape,
        scratch_shapes=[pltpu.VMEM((n_units, DN_HEAD_DIM, DN_HEAD_DIM), F32)],
        compiler_params=_cparams(("arbitrary",)),
        name="gdn_scan" if has_q else "gdn_scan_ctx",
    )(x, x, gates, gates, gates_t, gates_t, s0)


def _out_mlp_kernel(ff_tile, x_ref, oa_ref, of_ref, ob_ref, sz_ref, mod_ref, dnw_ref, wo_ref, n2w_ref,
                    w1_ref, w2_ref, o_ref):
    o_dn = of_ref[...] + ob_ref[...]
    parts = [oa_ref[...]]
    for h in range(DN_HEADS):
        cols = slice(h * DN_HEAD_DIM, (h + 1) * DN_HEAD_DIM)
        seg = o_dn[:, cols]
        seg = seg * lax.rsqrt(jnp.mean(seg * seg, axis=-1, keepdims=True) + NORM_EPS) * dnw_ref[...]
        parts.append((seg * sz_ref[:, cols]).astype(BF16))
    mixed = jnp.dot(jnp.concatenate(parts, axis=1), wo_ref[...], preferred_element_type=F32)
    x = x_ref[...] + mod_ref[2:3, :] * mixed

    y = x * lax.rsqrt(jnp.mean(x * x, axis=-1, keepdims=True) + NORM_EPS) * n2w_ref[...]
    h2 = (y * (1.0 + mod_ref[4:5, :]) + mod_ref[3:4, :]).astype(BF16)
    acc = None
    for c in range(w1_ref.shape[1] // ff_tile):
        a = jnp.dot(h2, w1_ref[:, c * ff_tile:(c + 1) * ff_tile], preferred_element_type=F32)
        a = jnp.square(jnp.maximum(a, 0.0)).astype(BF16)
        part = jnp.dot(a, w2_ref[c * ff_tile:(c + 1) * ff_tile, :], preferred_element_type=F32)
        acc = part if acc is None else acc + part
    o_ref[...] = x + mod_ref[5:6, :] * acc


def _out_mlp(x, o_attn, o_f, o_b, sz, mod, dnw, w_out, n2w, w1, w2, tm, ff_tile):
    b, t, d = x.shape
    row = lambda bi, i: (bi, i, 0)
    full = lambda bi, i: (0, 0)
    resident = lambda w: pl.BlockSpec(w.shape, full, pipeline_mode=pl.Buffered(1))
    return pl.pallas_call(
        functools.partial(_out_mlp_kernel, ff_tile),
        grid=(b, t // tm),
        in_specs=[pl.BlockSpec((None, tm, d), row),
                  pl.BlockSpec((None, tm, ATTN_W), row),
                  pl.BlockSpec((None, tm, DN_W), row),
                  pl.BlockSpec((None, tm, DN_W), row),
                  pl.BlockSpec((None, tm, DN_W), row),
                  pl.BlockSpec((None, N_MOD, d), lambda bi, i: (bi, 0, 0)),
                  pl.BlockSpec((1, DN_HEAD_DIM), full),
                  resident(w_out),
                  pl.BlockSpec((1, d), full),
                  resident(w1),
                  resident(w2)],
        out_specs=pl.BlockSpec((None, tm, d), row),
        out_shape=jax.ShapeDtypeStruct((b, t, d), F32),
        compiler_params=_cparams(("parallel", "parallel")),
        name="out_mlp",
    )(x, o_attn, o_f, o_b, sz, mod, dnw, w_out, n2w, w1, w2)


def _rope_tables(t):
    inv = ROPE_THETA ** (-jnp.arange(ROPE_HALF, dtype=F32) / ROPE_HALF)
    pos = jnp.arange(t, dtype=jnp.int32)
    ang_r = (pos // GRID_W).astype(F32)[:, None] * inv[None, :]
    ang_c = (pos % GRID_W).astype(F32)[:, None] * inv[None, :]
    cos = jnp.concatenate([jnp.cos(ang_r)] * 2 + [jnp.cos(ang_c)] * 2, axis=1)
    sin = jnp.concatenate([-jnp.sin(ang_r), jnp.sin(ang_r), -jnp.sin(ang_c), jnp.sin(ang_c)], axis=1)
    rep = LANES // ATTN_HEAD_DIM
    return jnp.tile(cos, (1, rep)), jnp.tile(sin, (1, rep))


def _pad_lanes(vec, offset):
    return jnp.zeros((1, GATE_W), F32).at[0, offset:offset + vec.size].set(vec.reshape(-1).astype(F32))


def kernel(x, c, ctx, c_ctx, w_mod, b_mod, norm1_w, w_in, q_norm_w, k_norm_w, conv_w, a_log, dt_bias,
           dn_norm_w, w_out, norm2_w, w_mlp1, w_mlp2):
    b, t, d = x.shape
    n_ctx = ctx.shape[1]
    depth = w_mod.shape[0]
    cos, sin = _rope_tables(t)
    head_id = jnp.arange(ATTN_W, dtype=jnp.int32) // ATTN_HEAD_DIM
    gmat = (head_id[:, None] == head_id[None, :]).astype(BF16)
    assert b < SUBLANES, "the modulation rows (batch + the context row) share one sublane tile"
    c_rows = jnp.zeros((SUBLANES, d), F32).at[:b].set(c).at[b].set(c_ctx)
    tm, tm_out = min(INPROJ_ROWS, t), min(OUT_ROWS, t)
    tq, tk = min(ATTN_Q_ROWS, t), min(ATTN_K_ROWS, t)
    for l in range(depth):
        mod = _modulation(c_rows, w_mod, b_mod[l], l).reshape(SUBLANES, N_MOD, d)
        w_in_p = jnp.pad(w_in[l], ((0, 0), (0, IN_COLS_PAD - w_in.shape[2]))).astype(BF16)
        consts = (norm1_w[l].reshape(1, d), w_in_p,
                  jnp.tile(q_norm_w[l], ATTN_HEADS).reshape(1, ATTN_W),
                  jnp.tile(k_norm_w[l], ATTN_KV_HEADS).reshape(1, ATTN_KV_W),
                  cos, sin, gmat, _pad_lanes(a_log[l], N_GATE), _pad_lanes(dt_bias[l], N_GATE))
        q_a, k_a, v_a, dqkv, sz, gates, gates_t = _inproj(
            x, mod, lambda bi, i: (bi, 0, 0), False, tm, consts + (conv_w[l],))
        k_c, v_c, ckv, cgates, cgates_t = _inproj(
            ctx, mod, lambda bi, i: (b, 0, 0), True, n_ctx, consts + (conv_w[l][:, DN_W:],))

        attn_args = (q_a, k_a, v_a, k_c, v_c, tq, tk)
        o_fixed, bad, qkv = _attention(False, *attn_args, dqkv=dqkv, conv_w=conv_w[l])
        o_attn = lax.cond(jnp.max(bad) > 0.0, lambda: _attention(True, *attn_args)[0], lambda: o_fixed)

        s_zero = jnp.zeros((b * N_GATE, DN_HEAD_DIM, DN_HEAD_DIM), F32)
        (s_ctx,) = _gdn_scan(ckv, cgates, cgates_t, s_zero, False, min(GDN_CHUNKS, n_ctx // CHUNK))
        o_f, o_b, _ = _gdn_scan(qkv, gates, gates_t, s_ctx, True, min(GDN_CHUNKS, t // CHUNK))

        x = _out_mlp(x, o_attn, o_f, o_b, sz, mod, dn_norm_w[l].reshape(1, DN_HEAD_DIM),
                     w_out[l].astype(BF16), norm2_w[l].reshape(1, d), w_mlp1[l].astype(BF16),
                     w_mlp2[l].astype(BF16), tm_out, MLP_FF_COLS)
    return x
```
